```python
import jax, jax.numpy as jnp
from jax import lax
import numpy as np


D_MODEL = 1024
BATCH = 8
SEQ = 4096
DEPTH = 2

CHUNK = 64
N_PREV_CHUNKS = 8
BAND = (N_PREV_CHUNKS + 1) * CHUNK
N_MEM = 256
MEM_HEADS = 4
MEM_HD = 128
MEM_W = MEM_HEADS * MEM_HD
ATT_HEADS = 16
ATT_HD = 64
ATT_W = ATT_HEADS * ATT_HD
D_RNN = 1280
RG_BLOCKS = 16
RG_BW = D_RNN // RG_BLOCKS
RG_C = 8.0
CONV_A = 4
D_FF = 3072
CONV_FFN = 3
MAX_REL = 256
N_REL = MAX_REL + CHUNK
EPS = 1e-6
N_A = DEPTH // 2
N_B = DEPTH - N_A

kernel_name = 'hybrid_rglru_chunkattn_yoco'


def rmsnorm(x, g):
    xf = x.astype(jnp.float32)
    y = xf * lax.rsqrt(jnp.mean(xf * xf, axis=-1, keepdims=True) + EPS)
    return (y * g.astype(jnp.float32)).astype(x.dtype)


def causal_dwconv(x, w, b):
    width = w.shape[0]
    y = lax.conv_general_dilated(
        x, w[:, None, :].astype(x.dtype), window_strides=(1,),
        padding=[(width - 1, 0)], dimension_numbers=('NWC', 'WIO', 'NWC'),
        feature_group_count=x.shape[-1])
    return y + b.astype(x.dtype)


def block_diag(x, w, b):
    bsz, s, _ = x.shape
    xb = x.reshape(bsz, s, RG_BLOCKS, RG_BW)
    return jnp.einsum('bsnk,nkj->bsnj', xb, w).reshape(bsz, s, D_RNN) + b


def rg_lru(x, w_r, b_r, w_i, b_i, lam):
    r = jax.nn.sigmoid(block_diag(x, w_r, b_r).astype(jnp.float32))
    i = jax.nn.sigmoid(block_diag(x, w_i, b_i).astype(jnp.float32))
    log_a = RG_C * r * jax.nn.log_sigmoid(lam.astype(jnp.float32))
    a = jnp.exp(log_a)
    u = jnp.sqrt(-jnp.expm1(2.0 * log_a)) * i * x.astype(jnp.float32)

    def combine(left, right):
        a_l, b_l = left
        a_r, b_r2 = right
        return a_l * a_r, a_r * b_l + b_r2

    _, h = lax.associative_scan(combine, (a, u), axis=1)
    return h.astype(x.dtype)


def mem_attention(q, mem_n, w_kv, g_q, g_k):
    bsz, s, _ = q.shape
    q = rmsnorm(q.reshape(bsz, s, MEM_HEADS, MEM_HD), g_q)
    kv = mem_n @ w_kv
    k, v = jnp.split(kv, 2, axis=-1)
    k = rmsnorm(k.reshape(bsz, -1, MEM_HEADS, MEM_HD), g_k)
    v = v.reshape(bsz, -1, MEM_HEADS, MEM_HD)
    sc = jnp.einsum('bshd,bmhd->bhsm', q, k).astype(jnp.float32) * (MEM_HD ** -0.5)
    p = jax.nn.softmax(sc, axis=-1).astype(v.dtype)
    return jnp.einsum('bhsm,bmhd->bshd', p, v).reshape(bsz, s, MEM_W)


def chunk_attention(q, k_pad, v_pad, bias_band):
    bsz, s, h, hd = q.shape
    n_chunks = s // CHUNK
    k_pos = jnp.arange(BAND)

    def one_chunk(c):
        qc = lax.dynamic_slice_in_dim(q, c * CHUNK, CHUNK, axis=1)
        kc = lax.dynamic_slice_in_dim(k_pad, c * CHUNK, BAND, axis=1)
        vc = lax.dynamic_slice_in_dim(v_pad, c * CHUNK, BAND, axis=1)
        sc = jnp.einsum('bqhd,bkhd->bhqk', qc, kc).astype(jnp.float32) * (hd ** -0.5) + bias_band
        valid = k_pos >= (N_PREV_CHUNKS - c) * CHUNK
        sc = jnp.where(valid[None, None, None, :], sc, -1e30)
        p = jax.nn.softmax(sc, axis=-1).astype(vc.dtype)
        return jnp.einsum('bhqk,bkhd->bqhd', p, vc)

    out = lax.map(one_chunk, jnp.arange(n_chunks))
    return out.transpose(1, 0, 2, 3, 4).reshape(bsz, s, h * hd)


def conv_ffn(x, w_up, w_dw, b_dw, w_down):
    hid = causal_dwconv(x @ w_up, w_dw, b_dw)
    u, g = jnp.split(hid, 2, axis=-1)
    return (jax.nn.gelu(g) * u) @ w_down


def setup_inputs(seed: int = 0) -> dict:
    key = jax.random.key(seed)
    ks = jax.random.split(key, 32)
    f32 = jnp.float32
    nrm = lambda k, shape, scale: scale * jax.random.normal(k, shape, f32)
    gain = lambda k, shape: 1.0 + 0.05 * jax.random.normal(k, shape, f32)
    u = jax.random.uniform(ks[13], (N_A, D_RNN), f32, minval=0.9, maxval=0.999)
    s_ = u ** (1.0 / RG_C)
    lam_a = jnp.log(s_) - jnp.log1p(-s_)
    return {
        'x': nrm(ks[0], (BATCH, SEQ, D_MODEL), 1.0),
        'mem': nrm(ks[1], (BATCH, N_MEM, D_MODEL), 1.0),
        'g_mix': gain(ks[2], (DEPTH, D_MODEL)),
        'g_mem': gain(ks[3], (DEPTH, D_MODEL)),
        'w_mem_kv': nrm(ks[4], (DEPTH, D_MODEL, 2 * MEM_W), D_MODEL ** -0.5),
        'g_q_mem': gain(ks[5], (DEPTH, MEM_HD)),
        'g_k_mem': gain(ks[6], (DEPTH, MEM_HD)),
        'w_in_a': nrm(ks[7], (N_A, D_MODEL, 2 * D_RNN + MEM_W), D_MODEL ** -0.5),
        'w_conv_a': nrm(ks[8], (N_A, CONV_A, D_RNN), CONV_A ** -0.5),
        'b_conv_a': nrm(ks[9], (N_A, D_RNN), 0.02),
        'w_r_a': nrm(ks[10], (N_A, RG_BLOCKS, RG_BW, RG_BW), RG_BW ** -0.5),
        'b_r_a': nrm(ks[11], (N_A, D_RNN), 0.02),
        'w_i_a': nrm(ks[12], (N_A, RG_BLOCKS, RG_BW, RG_BW), RG_BW ** -0.5),
        'b_i_a': nrm(ks[14], (N_A, D_RNN), 0.02),
        'lam_a': lam_a,
        'w_out_a': nrm(ks[15], (N_A, D_RNN + MEM_W, D_MODEL), (D_RNN + MEM_W) ** -0.5),
        'g_kv': gain(ks[16], (D_MODEL,)),
        'w_kv': nrm(ks[17], (D_MODEL, 2 * ATT_W), D_MODEL ** -0.5),
        'g_k_att': gain(ks[18], (ATT_HD,)),
        'w_in_b': nrm(ks[19], (N_B, D_MODEL, ATT_W + MEM_W), D_MODEL ** -0.5),
        'g_q_att': gain(ks[20], (N_B, ATT_HD)),
        'rel_bias_b': nrm(ks[21], (N_B, ATT_HEADS, N_REL), 0.2),
        'w_out_b': nrm(ks[22], (N_B, ATT_W + MEM_W, D_MODEL), (ATT_W + MEM_W) ** -0.5),
        'g_ffn': gain(ks[23], (DEPTH, D_MODEL)),
        'w_up': nrm(ks[24], (DEPTH, D_MODEL, 2 * D_FF), D_MODEL ** -0.5),
        'w_dw_ffn': nrm(ks[25], (DEPTH, CONV_FFN, 2 * D_FF), CONV_FFN ** -0.5),
        'b_dw_ffn': nrm(ks[26], (DEPTH, 2 * D_FF), 0.02),
        'w_down': nrm(ks[27], (DEPTH, D_FF, D_MODEL), D_FF ** -0.5),
    }


def reference(x, mem, g_mix, g_mem, w_mem_kv, g_q_mem, g_k_mem, w_in_a, w_conv_a, b_conv_a,
              w_r_a, b_r_a, w_i_a, b_i_a, lam_a, w_out_a, g_kv, w_kv, g_k_att, w_in_b,
              g_q_att, rel_bias_b, w_out_b, g_ffn, w_up, w_dw_ffn, b_dw_ffn, w_down):
    bsz, s, _ = x.shape
    rel = jnp.arange(CHUNK)[:, None] + N_PREV_CHUNKS * CHUNK - jnp.arange(BAND)[None, :]
    rel_idx = jnp.clip(rel, -(CHUNK - 1), MAX_REL) + (CHUNK - 1)
    k_pad = None
    v_pad = None
    for l in range(DEPTH):
        h = rmsnorm(x, g_mix[l])
        mem_n = rmsnorm(mem, g_mem[l])
        if l < N_A:
            proj = h @ w_in_a[l]
            xr, yg, qm = jnp.split(proj, [D_RNN, 2 * D_RNN], axis=-1)
            xr = causal_dwconv(xr, w_conv_a[l], b_conv_a[l])
            main = rg_lru(xr, w_r_a[l], b_r_a[l], w_i_a[l], b_i_a[l], lam_a[l]) * jax.nn.gelu(yg)
            w_out = w_out_a[l]
        else:
            j = l - N_A
            if j == 0:
                kv = rmsnorm(x, g_kv) @ w_kv
                k, v = jnp.split(kv, 2, axis=-1)
                k = rmsnorm(k.reshape(bsz, s, ATT_HEADS, ATT_HD), g_k_att)
                v = v.reshape(bsz, s, ATT_HEADS, ATT_HD)
                pad = ((0, 0), (N_PREV_CHUNKS * CHUNK, 0), (0, 0), (0, 0))
                k_pad = jnp.pad(k, pad)
                v_pad = jnp.pad(v, pad)
            proj = h @ w_in_b[j]
            qa, qm = jnp.split(proj, [ATT_W], axis=-1)
            qa = rmsnorm(qa.reshape(bsz, s, ATT_HEADS, ATT_HD), g_q_att[j])
            bias_band = rel_bias_b[j][:, rel_idx].astype(jnp.float32)
            main = chunk_attention(qa, k_pad, v_pad, bias_band)
            w_out = w_out_b[j]
        mo = mem_attention(qm, mem_n, w_mem_kv[l], g_q_mem[l], g_k_mem[l])
        x = x + jnp.concatenate([main, mo], axis=-1) @ w_out
        x = x + conv_ffn(rmsnorm(x, g_ffn[l]), w_up[l], w_dw_ffn[l], b_dw_ffn[l], w_down[l])
    return x
```

```python
import functools
import math

import jax
import jax.numpy as jnp
from jax import lax
from jax.experimental import pallas as pl
from jax.experimental.pallas import tpu as pltpu

EPS = 1e-6
CHUNK = 64
N_PREV_CHUNKS = 8
BAND = (N_PREV_CHUNKS + 1) * CHUNK
MAX_REL = 256
RG_C = 8.0
NEG = -1e30

LANES = 128
SUBLANES = 8
VMEM_LIMIT = 56 * 1024 * 1024

F32 = jnp.float32
BF16 = jnp.bfloat16


def _cparams(sem):
    return pltpu.CompilerParams(dimension_semantics=sem, vmem_limit_bytes=VMEM_LIMIT)


def _dot(a, b):
    return jnp.dot(a, b, preferred_element_type=F32)


def _dot_nt(a, b):
    return lax.dot_general(a, b, (((1,), (1,)), ((), ())), preferred_element_type=F32)


def _rms(x, g):
    ms = jnp.mean(x * x, axis=-1, keepdims=True)
    return x * lax.rsqrt(ms + EPS) * g


def _shift_rows(x, prev, d):
    n = x.shape[0]
    xc = jnp.concatenate([prev, x], axis=0)
    return pltpu.roll(xc, d, axis=0)[SUBLANES:SUBLANES + n]


def _head_rms_rows(x, g, hd):
    t, w = x.shape
    lane = lax.broadcasted_iota(jnp.int32, (t, LANES), 1)
    lo = lane < hd
    outs = []
    for p in range(w // LANES):
        xp = x[:, p * LANES:(p + 1) * LANES]
        x2 = xp * xp
        s_lo = jnp.sum(jnp.where(lo, x2, 0.0), axis=-1, keepdims=True)
        s_hi = jnp.sum(jnp.where(lo, 0.0, x2), axis=-1, keepdims=True)
        r_lo = lax.rsqrt(s_lo * (1.0 / hd) + EPS)
        r_hi = lax.rsqrt(s_hi * (1.0 / hd) + EPS)
        outs.append(xp * jnp.where(lo, r_lo, r_hi))
    return jnp.concatenate(outs, axis=-1) * g


def _mem_attention(qm, km_ref, vm_ref, gq, n_heads, hd):
    outs = []
    for hh in range(n_heads):
        sl = slice(hh * hd, (hh + 1) * hd)
        qn = (_rms(qm[:, sl], gq) * (hd ** -0.5)).astype(BF16)
        s = _dot_nt(qn, km_ref[:, sl])
        m = jnp.max(s, axis=-1, keepdims=True)
        e = jnp.exp(s - m)
        l = jnp.sum(e, axis=-1, keepdims=True)
        o = _dot(e.astype(BF16), vm_ref[:, sl])
        outs.append(o / l)
    return jnp.concatenate(outs, axis=-1)


def _memkv_kernel(mem_ref, g_ref, w_ref, gk_ref, km_ref, vm_ref, *, n_heads, hd):
    mn = _rms(mem_ref[...], g_ref[...]).astype(BF16)
    kv = _dot(mn, w_ref[...])
    w = n_heads * hd
    ks = []
    for hh in range(n_heads):
        ks.append(_rms(kv[:, hh * hd:(hh + 1) * hd], gk_ref[...]))
    km_ref[...] = jnp.concatenate(ks, axis=-1).astype(BF16)
    vm_ref[...] = kv[:, w:].astype(BF16)


def _memkv_call(mem, g_mem, w_mem_kv, g_k_mem, n_heads, hd):
    depth = g_mem.shape[0]
    bsz, n_mem, d = mem.shape
    w = n_heads * hd
    out = jax.ShapeDtypeStruct((depth, bsz, n_mem, w), BF16)
    return pl.pallas_call(
        functools.partial(_memkv_kernel, n_heads=n_heads, hd=hd),
        grid=(depth, bsz),
        in_specs=[
            pl.BlockSpec((None, n_mem, d), lambda l, b: (b, 0, 0)),
            pl.BlockSpec((None, 1, d), lambda l, b: (l, 0, 0)),
            pl.BlockSpec((None, d, 2 * w), lambda l, b: (l, 0, 0)),
            pl.BlockSpec((None, 1, hd), lambda l, b: (l, 0, 0)),
        ],
        out_specs=[
            pl.BlockSpec((None, None, n_mem, w), lambda l, b: (l, b, 0, 0)),
            pl.BlockSpec((None, None, n_mem, w), lambda l, b: (l, b, 0, 0)),
        ],
        out_shape=[out, out],
        compiler_params=_cparams(("arbitrary", "arbitrary")),
        name="mem_kv",
    )(mem, g_mem.reshape(depth, 1, d), w_mem_kv.astype(BF16), g_k_mem.reshape(depth, 1, hd))


def _mixer_a_kernel(x_ref, g_ref, win_ref, wc_ref, bc_ref, wbd_ref, br_ref, bi_ref, lam_ref,
                    km_ref, vm_ref, gq_ref, wout_ref, o_ref, xprev_ref, hprev_ref,
                    *, d_rnn, n_heads, hd):
    s_idx = pl.program_id(1)

    @pl.when(s_idx == 0)
    def _():
        xprev_ref[...] = jnp.zeros_like(xprev_ref)
        hprev_ref[...] = jnp.zeros_like(hprev_ref)

    x = x_ref[...]
    ts = x.shape[0]
    h = _rms(x, g_ref[...]).astype(BF16)
    proj = _dot(h, win_ref[...])
    xr = proj[:, :d_rnn]
    yg = proj[:, d_rnn:2 * d_rnn]
    qm = proj[:, 2 * d_rnn:]

    prev = xprev_ref[...]
    width = wc_ref.shape[0]
    conv = xr * wc_ref[width - 1:width, :] + bc_ref[...]
    for k in range(width - 1):
        conv = conv + _shift_rows(xr, prev, width - 1 - k) * wc_ref[k:k + 1, :]
    xprev_ref[...] = xr[ts - SUBLANES:, :]

    half = d_rnn // 2
    cb = conv.astype(BF16)
    ri0 = _dot(cb[:, :half], wbd_ref[0])
    ri1 = _dot(cb[:, half:], wbd_ref[1])
    r = jax.nn.sigmoid(jnp.concatenate([ri0[:, :half], ri1[:, :half]], axis=-1) + br_ref[...])
    ig = jax.nn.sigmoid(jnp.concatenate([ri0[:, half:], ri1[:, half:]], axis=-1) + bi_ref[...])
    lam = lam_ref[...]
    log_sig = jnp.minimum(lam, 0.0) - jnp.log1p(jnp.exp(-jnp.abs(lam)))
    log_a = RG_C * r * log_sig
    a = jnp.exp(log_a)
    u = jnp.sqrt(-jnp.tanh(log_a) * (a * a + 1.0)) * ig * conv

    row = lax.broadcasted_iota(jnp.int32, a.shape, 0)
    d = 1
    while d < ts:
        if d < SUBLANES:
            a_sh = jnp.where(row < d, 1.0, pltpu.roll(a, d, axis=0))
            u_sh = jnp.where(row < d, 0.0, pltpu.roll(u, d, axis=0))
        else:
            a_sh = jnp.concatenate([jnp.ones((d, d_rnn), F32), a[:ts - d]], axis=0)
            u_sh = jnp.concatenate([jnp.zeros((d, d_rnn), F32), u[:ts - d]], axis=0)
        u = a * u_sh + u
        a = a * a_sh
        d *= 2
    hs = a * hprev_ref[SUBLANES - 1:SUBLANES, :] + u
    hprev_ref[...] = hs[ts - SUBLANES:, :]

    main = (hs * jax.nn.gelu(yg)).astype(BF16)
    mo = _mem_attention(qm, km_ref, vm_ref, gq_ref[...], n_heads, hd).astype(BF16)
    y = _dot(main, wout_ref[:d_rnn, :]) + _dot(mo, wout_ref[d_rnn:, :])
    o_ref[...] = x + y


def _mixer_a_call(x, g, w_in, w_conv, b_conv, wbd, b_r, b_i, lam, km, vm, gq, w_out, ts, n_heads, hd):
    bsz, s, d = x.shape
    d_rnn = w_conv.shape[1]
    n_mem, mw = km.shape[1:]
    const = lambda *shape: pl.BlockSpec(shape, lambda b, i: (0,) * len(shape))
    return pl.pallas_call(
        functools.partial(_mixer_a_kernel, d_rnn=d_rnn, n_heads=n_heads, hd=hd),
        grid=(bsz, s // ts),
        in_specs=[
            pl.BlockSpec((None, ts, d), lambda b, i: (b, i, 0)),
            const(1, d),
            const(*w_in.shape),
            const(*w_conv.shape),
            const(1, d_rnn),
            const(*wbd.shape),
            const(1, d_rnn),
            const(1, d_rnn),
            const(1, d_rnn),
            pl.BlockSpec((None, n_mem, mw), lambda b, i: (b, 0, 0)),
            pl.BlockSpec((None, n_mem, mw), lambda b, i: (b, 0, 0)),
            const(1, hd),
            const(*w_out.shape),
        ],
        out_specs=pl.BlockSpec((None, ts, d), lambda b, i: (b, i, 0)),
        out_shape=jax.ShapeDtypeStruct(x.shape, F32),
        scratch_shapes=[pltpu.VMEM((SUBLANES, d_rnn), F32), pltpu.VMEM((SUBLANES, d_rnn), F32)],
        compiler_params=_cparams(("arbitrary", "arbitrary")),
        name="mixer_a",
    )(x, g, w_in, w_conv, b_conv, wbd, b_r, b_i, lam, km, vm, gq, w_out)


def _ffn_kernel(x_ref, g_ref, wu_ref, wg_ref, cu_ref, cg_ref, bu_ref, bg_ref, wd_ref, o_ref,
                h_ref, acc_ref, carry_ref):
    s_idx = pl.program_id(1)
    k = pl.program_id(2)
    nk = pl.num_programs(2)

    @pl.when(k == 0)
    def _():
        h_ref[...] = _rms(x_ref[...], g_ref[...]).astype(BF16)

    @pl.when(s_idx == 0)
    def _():
        carry_ref[k] = jnp.zeros(carry_ref.shape[1:], F32)

    hb = h_ref[...]
    ts = hb.shape[0]
    width = cu_ref.shape[0]

    def branch(w_ref, c_ref, b_ref, slot):
        pre = _dot(hb, w_ref[...])
        prev = carry_ref[k, slot]
        out = pre * c_ref[width - 1:width, :] + b_ref[...]
        for j in range(width - 1):
            out = out + _shift_rows(pre, prev, width - 1 - j) * c_ref[j:j + 1, :]
        carry_ref[k, slot] = pre[ts - SUBLANES:, :]
        return out

    u = branch(wu_ref, cu_ref, bu_ref, 0)
    gt = branch(wg_ref, cg_ref, bg_ref, 1)
    act = (jax.nn.gelu(gt) * u).astype(BF16)
    part = _dot(act, wd_ref[...])

    @pl.when(k == 0)
    def _():
        acc_ref[...] = part

    @pl.when(jnp.logical_and(k > 0, k < nk - 1))
    def _():
        acc_ref[...] += part

    @pl.when(k == nk - 1)
    def _():
        o_ref[...] = x_ref[...] + acc_ref[...] + part


def _ffn_call(x, g, w_up, w_dw, b_dw, w_down, ts, fc):
    bsz, s, d = x.shape
    d_ff = w_down.shape[0]
    nk = d_ff // fc
    assert nk >= 2
    width = w_dw.shape[0]
    return pl.pallas_call(
        _ffn_kernel,
        grid=(bsz, s // ts, nk),
        in_specs=[
            pl.BlockSpec((None, ts, d), lambda b, i, k: (b, i, 0)),
            pl.BlockSpec((1, d), lambda b, i, k: (0, 0)),
            pl.BlockSpec((d, fc), lambda b, i, k: (0, k)),
            pl.BlockSpec((d, fc), lambda b, i, k: (0, nk + k)),
            pl.BlockSpec((width, fc), lambda b, i, k: (0, k)),
            pl.BlockSpec((width, fc), lambda b, i, k: (0, nk + k)),
            pl.BlockSpec((1, fc), lambda b, i, k: (0, k)),
            pl.BlockSpec((1, fc), lambda b, i, k: (0, nk + k)),
            pl.BlockSpec((fc, d), lambda b, i, k: (k, 0)),
        ],
        out_specs=pl.BlockSpec((None, ts, d), lambda b, i, k: (b, i, 0)),
        out_shape=jax.ShapeDtypeStruct(x.shape, F32),
        scratch_shapes=[
            pltpu.VMEM((ts, d), BF16),
            pltpu.VMEM((ts, d), F32),
            pltpu.VMEM((nk, 2, SUBLANES, fc), F32),
        ],
        compiler_params=_cparams(("arbitrary", "arbitrary", "arbitrary")),
        name="conv_ffn",
    )(x, g, w_up, w_up, w_dw, w_dw, b_dw, b_dw, w_down)


def _kv_kernel(x_ref, g_ref, wk_ref, wvt_ref, gk_ref, k_ref, vt_ref, *, hd):
    h = _rms(x_ref[...], g_ref[...]).astype(BF16)
    k = _dot(h, wk_ref[...])
    k_ref[...] = _head_rms_rows(k, gk_ref[...], hd).astype(BF16)
    vt_ref[...] = _dot_nt(wvt_ref[...], h).astype(BF16)


def _kv_call(x, g, w_k, w_vt, gk_row, ts, hd):
    bsz, s, d = x.shape
    w = w_k.shape[1]
    return pl.pallas_call(
        functools.partial(_kv_kernel, hd=hd),
        grid=(bsz, s // ts),
        in_specs=[
            pl.BlockSpec((None, ts, d), lambda b, i: (b, i, 0)),
            pl.BlockSpec((1, d), lambda b, i: (0, 0)),
            pl.BlockSpec((d, w), lambda b, i: (0, 0)),
            pl.BlockSpec((w, d), lambda b, i: (0, 0)),
            pl.BlockSpec((1, w), lambda b, i: (0, 0)),
        ],
        out_specs=[
            pl.BlockSpec((None, ts, w), lambda b, i: (b, i, 0)),
            pl.BlockSpec((None, w, ts), lambda b, i: (b, 0, i)),
        ],
        out_shape=[jax.ShapeDtypeStruct((bsz, s, w), BF16), jax.ShapeDtypeStruct((bsz, w, s), BF16)],
        compiler_params=_cparams(("arbitrary", "arbitrary")),
        name="kv_proj",
    )(x, g, w_k, w_vt, gk_row)


def _mixer_b_kernel(x_ref, g_ref, win_ref, gqa_ref, k0_ref, k1_ref, k2_ref, v0_ref, v1_ref, v2_ref,
                    pb_ref, km_ref, vm_ref, gq_ref, wout_ref, o_ref, qa_ref, att_ref,
                    *, att_w, att_hd, n_heads, hd):
    i = pl.program_id(1)
    x = x_ref[...]
    tq = x.shape[0]
    half = tq // 2
    h = _rms(x, g_ref[...]).astype(BF16)
    proj = _dot(h, win_ref[...])
    qa_ref[...] = _head_rms_rows(proj[:, :att_w], gqa_ref[...], att_hd).astype(BF16)
    qm = proj[:, att_w:]

    k_refs = (k0_ref, k1_ref, k2_ref)
    v_refs = (v0_ref, v1_ref, v2_ref)
    ok0 = i >= 2
    ok1 = i >= 1
    lane = lax.broadcasted_iota(jnp.int32, (tq, LANES), 1)
    zeros = jnp.zeros((half, half), F32)

    def pair_body(p, carry):
        col = pl.multiple_of(p * LANES, LANES)
        q_pair = qa_ref[:, pl.ds(col, LANES)]
        for sub in range(2):
            hh = 2 * p + sub
            keep = (lane < att_hd) if sub == 0 else (lane >= att_hd)
            qz = jnp.where(keep, q_pair, jnp.zeros_like(q_pair))
            st = [_dot_nt(kr[:, pl.ds(col, LANES)], qz) for kr in k_refs]
            pb = pb_ref[hh]
            left = jnp.concatenate([
                jnp.where(ok0, st[0][:, :half], NEG),
                jnp.where(ok1, st[1][:, :half], NEG),
                st[2][:half, :half]], axis=0) + pb
            right = jnp.concatenate([
                jnp.where(ok0, st[0][half:, half:], NEG),
                jnp.where(ok1, st[1][:, half:], NEG),
                st[2][:, half:]], axis=0) + pb
            m_l = jnp.max(left, axis=0, keepdims=True)
            m_r = jnp.max(right, axis=0, keepdims=True)
            e_l = jnp.exp(left - m_l)
            e_r = jnp.exp(right - m_r)
            l_sum = jnp.concatenate([jnp.sum(e_l, axis=0, keepdims=True),
                                     jnp.sum(e_r, axis=0, keepdims=True)], axis=1)
            pt = [
                jnp.concatenate([e_l[:tq], jnp.concatenate([zeros, e_r[:half]], axis=0)], axis=1),
                jnp.concatenate([e_l[tq:2 * tq], e_r[half:half + tq]], axis=1),
                jnp.concatenate([jnp.concatenate([e_l[2 * tq:], zeros], axis=0), e_r[half + tq:]], axis=1),
            ]
            row0 = pl.multiple_of(hh * att_hd, att_hd)
            ot = _dot(v_refs[0][pl.ds(row0, att_hd), :], pt[0].astype(BF16))
            ot = ot + _dot(v_refs[1][pl.ds(row0, att_hd), :], pt[1].astype(BF16))
            ot = ot + _dot(v_refs[2][pl.ds(row0, att_hd), :], pt[2].astype(BF16))
            att_ref[pl.ds(row0, att_hd), :] = ot / l_sum
        return carry

    lax.fori_loop(0, att_w // LANES, pair_body, 0)

    att = att_ref[...].T.astype(BF16)
    mo = _mem_attention(qm, km_ref, vm_ref, gq_ref[...], n_heads, hd).astype(BF16)
    y = _dot(att, wout_ref[:att_w, :]) + _dot(mo, wout_ref[att_w:, :])
    o_ref[...] = x + y


def _mixer_b_call(x, g, w_in, gqa_row, k, vt, pb, km, vm, gq, w_out, tq, att_hd, n_heads, hd):
    bsz, s, d = x.shape
    att_w = k.shape[2]
    n_mem, mw = km.shape[1:]
    const = lambda *shape: pl.BlockSpec(shape, lambda b, i: (0,) * len(shape))
    kspec = lambda r: pl.BlockSpec((None, tq, att_w), lambda b, i: (b, jnp.maximum(i - 2 + r, 0), 0))
    vspec = lambda r: pl.BlockSpec((None, att_w, tq), lambda b, i: (b, 0, jnp.maximum(i - 2 + r, 0)))
    return pl.pallas_call(
        functools.partial(_mixer_b_kernel, att_w=att_w, att_hd=att_hd, n_heads=n_heads, hd=hd),
        grid=(bsz, s // tq),
        in_specs=[
            pl.BlockSpec((None, tq, d), lambda b, i: (b, i, 0)),
            const(1, d),
            const(*w_in.shape),
            const(1, att_w),
            kspec(0), kspec(1), kspec(2),
            vspec(0), vspec(1), vspec(2),
            const(*pb.shape),
            pl.BlockSpec((None, n_mem, mw), lambda b, i: (b, 0, 0)),
            pl.BlockSpec((None, n_mem, mw), lambda b, i: (b, 0, 0)),
            const(1, hd),
            const(*w_out.shape),
        ],
        out_specs=pl.BlockSpec((None, tq, d), lambda b, i: (b, i, 0)),
        out_shape=jax.ShapeDtypeStruct(x.shape, F32),
        scratch_shapes=[pltpu.VMEM((tq, att_w), BF16), pltpu.VMEM((att_w, tq), F32)],
        compiler_params=_cparams(("arbitrary", "arbitrary")),
        name="mixer_b",
    )(x, g, w_in, gqa_row, k, k, k, vt, vt, vt, pb, km, vm, gq, w_out)


def _block_diag_halves(w_r, w_i):
    n, bw, _ = w_r.shape
    half_blocks = n // 2

    def dense_half(w, j):
        blocks = w[j * half_blocks:(j + 1) * half_blocks]
        eye = jnp.eye(half_blocks, dtype=w.dtype)
        return jnp.einsum('nkj,nm->nkmj', blocks, eye).reshape(half_blocks * bw, half_blocks * bw)

    halves = [jnp.concatenate([dense_half(w_r, j), dense_half(w_i, j)], axis=1) for j in range(2)]
    return jnp.stack(halves).astype(BF16)


def _pair_bias(rel_bias):
    rel = jnp.arange(CHUNK)[:, None] + N_PREV_CHUNKS * CHUNK - jnp.arange(BAND)[None, :]
    rel_idx = jnp.clip(rel, -(CHUNK - 1), MAX_REL) + (CHUNK - 1)
    band_t = jnp.transpose(rel_bias[:, rel_idx].astype(F32), (0, 2, 1))
    lo = jnp.pad(band_t, ((0, 0), (0, CHUNK), (0, 0)), constant_values=NEG)
    hi = jnp.pad(band_t, ((0, 0), (CHUNK, 0), (0, 0)), constant_values=NEG)
    return jnp.concatenate([lo, hi], axis=-1)


def kernel(x, mem, g_mix, g_mem, w_mem_kv, g_q_mem, g_k_mem, w_in_a, w_conv_a, b_conv_a, w_r_a, b_r_a,
           w_i_a, b_i_a, lam_a, w_out_a, g_kv, w_kv, g_k_att, w_in_b, g_q_att, rel_bias_b, w_out_b,
           g_ffn, w_up, w_dw_ffn, b_dw_ffn, w_down):
    bsz, s, d = x.shape
    depth = g_mix.shape[0]
    n_a = w_in_a.shape[0]
    mem_hd = g_q_mem.shape[1]
    mem_w = w_mem_kv.shape[2] // 2
    mem_heads = mem_w // mem_hd
    att_hd = g_k_att.shape[0]
    att_w = w_kv.shape[1] // 2
    att_heads = att_w // att_hd
    d_rnn = w_conv_a.shape[2]
    assert att_hd * 2 == LANES and CHUNK * 4 == 256 and s % 512 == 0

    row = lambda v: v.reshape(1, -1).astype(F32)
    km_all, vm_all = _memkv_call(mem, g_mem, w_mem_kv, g_k_mem, mem_heads, mem_hd)

    k = vt = None
    for l in range(depth):
        if l < n_a:
            wbd = _block_diag_halves(w_r_a[l], w_i_a[l])
            x = _mixer_a_call(x, row(g_mix[l]), w_in_a[l].astype(BF16), w_conv_a[l], row(b_conv_a[l]), wbd,
                              row(b_r_a[l]), row(b_i_a[l]), row(lam_a[l]), km_all[l], vm_all[l],
                              row(g_q_mem[l]), w_out_a[l].astype(BF16), 256, mem_heads, mem_hd)
        else:
            j = l - n_a
            if j == 0:
                k, vt = _kv_call(x, row(g_kv), w_kv[:, :att_w].astype(BF16), w_kv[:, att_w:].T.astype(BF16),
                                 row(jnp.tile(g_k_att, att_heads)), 512, att_hd)
            gqa = row(jnp.tile(g_q_att[j], att_heads) * (att_hd ** -0.5))
            x = _mixer_b_call(x, row(g_mix[l]), w_in_b[j].astype(BF16), gqa, k, vt, _pair_bias(rel_bias_b[j]),
                              km_all[l], vm_all[l], row(g_q_mem[l]), w_out_b[j].astype(BF16),
                              256, att_hd, mem_heads, mem_hd)
        x = _ffn_call(x, row(g_ffn[l]), w_up[l].astype(BF16), w_dw_ffn[l], row(b_dw_ffn[l]),
                      w_down[l].astype(BF16), 512, 1024)
    return x
```

```python
import functools
import math

import jax
import jax.numpy as jnp
from jax import lax
from jax.experimental import pallas as pl
from jax.experimental.pallas import tpu as pltpu

EPS = 1e-6
CHUNK = 64
N_PREV_CHUNKS = 8
BAND = (N_PREV_CHUNKS + 1) * CHUNK
MAX_REL = 256
RG_C = 8.0
NEG = -1e30
LOG2E = math.log2(math.e)

LANES = 128
SUBLANES = 8
BF16_ROWS = 16
VMEM_LIMIT = 56 * 1024 * 1024

TT_RGLRU = 32
TT_FFN = 64
FC_FFN = 1024
TS_MIX = 512
TS_KV = 512
TQ = 4 * CHUNK
KEY_ROWS = BAND + CHUNK
NEG_ROWS = TQ

F32 = jnp.float32
BF16 = jnp.bfloat16


def _cparams(sem):
    return pltpu.CompilerParams(dimension_semantics=sem, vmem_limit_bytes=VMEM_LIMIT)


def _dot(a, b):
    return jnp.dot(a, b, preferred_element_type=F32)


def _dot_nt(a, b):
    return lax.dot_general(a, b, (((1,), (1,)), ((), ())), preferred_element_type=F32)


def _rms(x, g):
    ms = jnp.mean(x * x, axis=-1, keepdims=True)
    return x * lax.rsqrt(ms + EPS) * g


def _gelu(x):
    z2 = x * (2.0 * 0.7978845608028654 + (2.0 * 0.7978845608028654 * 0.044715) * (x * x))
    return x / (1.0 + jnp.exp(-z2))


def _to_time_major(x_ref, pad_ref):
    nb, tt, d = x_ref.shape
    pitch = pad_ref.shape[1] // nb
    for b in range(nb):
        for c in range(d // LANES):
            pad_ref[c, b * pitch:b * pitch + tt, :] = x_ref[b, :, c * LANES:(c + 1) * LANES].astype(F32)
    rows = []
    for t in range(tt):
        rows.append(jnp.concatenate(
            [pad_ref[c, pl.ds(t, nb, stride=pitch), :] for c in range(d // LANES)], axis=1))
    return jnp.concatenate(rows, axis=0)


def _from_time_major(y, o_ref, pad_ref, res_ref=None):
    nb, tt, d = o_ref.shape
    pitch = pad_ref.shape[1] // nb
    for t in range(tt):
        for c in range(d // LANES):
            pad_ref[c, pl.ds(t, nb, stride=pitch), :] = y[t * nb:(t + 1) * nb, c * LANES:(c + 1) * LANES]
    for b in range(nb):
        for c in range(d // LANES):
            v = pad_ref[c, b * pitch:b * pitch + tt, :]
            if res_ref is not None:
                v = v + res_ref[b, :, c * LANES:(c + 1) * LANES]
            o_ref[b, :, c * LANES:(c + 1) * LANES] = v.astype(o_ref.dtype)


def _causal_conv_tm(x, prev, w_ref, b_ref, nb):
    n = x.shape[0]
    width = w_ref.shape[0]
    xc = jnp.concatenate([prev, x], axis=0)
    out = x * w_ref[width - 1:width, :] + b_ref[...]
    for k in range(width - 1):
        out = out + xc[k * nb:k * nb + n] * w_ref[k:k + 1, :]
    return out


def _head_rms_rows(x, g, hd):
    t, w = x.shape
    lane = lax.broadcasted_iota(jnp.int32, (t, LANES), 1)
    lo = lane < hd
    outs = []
    for p in range(w // LANES):
        xp = x[:, p * LANES:(p + 1) * LANES]
        x2 = xp * xp
        s_lo = jnp.sum(jnp.where(lo, x2, 0.0), axis=-1, keepdims=True)
        s_hi = jnp.sum(jnp.where(lo, 0.0, x2), axis=-1, keepdims=True)
        r_lo = lax.rsqrt(s_lo * (1.0 / hd) + EPS)
        r_hi = lax.rsqrt(s_hi * (1.0 / hd) + EPS)
        outs.append(xp * jnp.where(lo, r_lo, r_hi))
    return jnp.concatenate(outs, axis=-1) * g


def _mem_attention(qm, km_ref, vm_ref, gq, n_heads, hd):
    outs = []
    for hh in range(n_heads):
        sl = slice(hh * hd, (hh + 1) * hd)
        qn = (_rms(qm[:, sl], gq) * (hd ** -0.5)).astype(BF16)
        s = _dot_nt(qn, km_ref[:, sl])
        m = jnp.max(s, axis=-1, keepdims=True)
        e = jnp.exp(s - m)
        l = jnp.sum(e, axis=-1, keepdims=True)
        o = _dot(e.astype(BF16), vm_ref[:, sl])
        outs.append(o / l)
    return jnp.concatenate(outs, axis=-1)


def _memkv_kernel(mem_ref, g_ref, w_ref, gk_ref, km_ref, vm_ref, *, n_heads, hd):
    mn = _rms(mem_ref[...], g_ref[...]).astype(BF16)
    kv = _dot(mn, w_ref[...])
    w = n_heads * hd
    ks = []
    for hh in range(n_heads):
        ks.append(_rms(kv[:, hh * hd:(hh + 1) * hd], gk_ref[...]))
    km_ref[...] = jnp.concatenate(ks, axis=-1).astype(BF16)
    vm_ref[...] = kv[:, w:].astype(BF16)


def _memkv_call(mem, g_mem, w_mem_kv, g_k_mem, n_heads, hd):
    depth = g_mem.shape[0]
    bsz, n_mem, d = mem.shape
    w = n_heads * hd
    out = jax.ShapeDtypeStruct((depth, bsz, n_mem, w), BF16)
    return pl.pallas_call(
        functools.partial(_memkv_kernel, n_heads=n_heads, hd=hd),
        grid=(depth, bsz),
        in_specs=[
            pl.BlockSpec((None, n_mem, d), lambda l, b: (b, 0, 0)),
            pl.BlockSpec((None, 1, d), lambda l, b: (l, 0, 0)),
            pl.BlockSpec((None, d, 2 * w), lambda l, b: (l, 0, 0)),
            pl.BlockSpec((None, 1, hd), lambda l, b: (l, 0, 0)),
        ],
        out_specs=[
            pl.BlockSpec((None, None, n_mem, w), lambda l, b: (l, b, 0, 0)),
            pl.BlockSpec((None, None, n_mem, w), lambda l, b: (l, b, 0, 0)),
        ],
        out_shape=[out, out],
        compiler_params=_cparams(("arbitrary", "arbitrary")),
        name="mem_kv",
    )(mem, g_mem.reshape(depth, 1, d), w_mem_kv.astype(BF16), g_k_mem.reshape(depth, 1, hd))


def _rglru_kernel(x_ref, g_ref, win_ref, wc_ref, bc_ref, wbd_ref, br_ref, bi_ref, lam_ref, o_ref,
                  pad_ref, xprev_ref, hprev_ref):
    @pl.when(pl.program_id(0) == 0)
    def _():
        xprev_ref[...] = jnp.zeros_like(xprev_ref)
        hprev_ref[...] = jnp.zeros_like(hprev_ref)

    nb, tt, _ = x_ref.shape
    d_rnn = wc_ref.shape[1]
    x = _to_time_major(x_ref, pad_ref)
    n = x.shape[0]
    h = _rms(x, g_ref[...]).astype(BF16)
    proj = _dot(h, win_ref[...])
    xr = proj[:, :d_rnn]
    yg = proj[:, d_rnn:]

    conv = _causal_conv_tm(xr, xprev_ref[...], wc_ref, bc_ref, nb)
    xprev_ref[...] = xr[n - xprev_ref.shape[0]:, :]

    half = d_rnn // 2
    cb = conv.astype(BF16)
    ri0 = _dot(cb[:, :half], wbd_ref[0])
    ri1 = _dot(cb[:, half:], wbd_ref[1])
    r = jax.nn.sigmoid(jnp.concatenate([ri0[:, :half], ri1[:, :half]], axis=-1) + br_ref[...])
    ig = jax.nn.sigmoid(jnp.concatenate([ri0[:, half:], ri1[:, half:]], axis=-1) + bi_ref[...])
    lam = lam_ref[...]
    log_sig = jnp.minimum(lam, 0.0) - jnp.log1p(jnp.exp(-jnp.abs(lam)))
    log_a = RG_C * r * log_sig
    a = jnp.exp(log_a)
    z = -jnp.tanh(log_a) * (a * a + 1.0)
    u = jnp.where(z > 0.0, z * lax.rsqrt(z), 0.0) * ig * conv

    hcur = hprev_ref[...]
    hs = []
    for t in range(tt):
        hcur = a[t * nb:(t + 1) * nb] * hcur + u[t * nb:(t + 1) * nb]
        hs.append(hcur)
    hprev_ref[...] = hcur
    main = jnp.concatenate(hs, axis=0) * _gelu(yg)
    _from_time_major(main, o_ref, pad_ref)


def _rglru_call(x, g, w_in_xy, w_conv, b_conv, wbd, b_r, b_i, lam):
    bsz, s, d = x.shape
    d_rnn = w_conv.shape[1]
    tt = TT_RGLRU
    const = lambda *shape: pl.BlockSpec(shape, lambda i: (0,) * len(shape))
    return pl.pallas_call(
        _rglru_kernel,
        grid=(s // tt,),
        in_specs=[
            pl.BlockSpec((bsz, tt, d), lambda i: (0, i, 0)),
            const(1, d),
            const(*w_in_xy.shape),
            const(*w_conv.shape),
            const(1, d_rnn),
            const(*wbd.shape),
            const(1, d_rnn),
            const(1, d_rnn),
            const(1, d_rnn),
        ],
        out_specs=pl.BlockSpec((bsz, tt, d_rnn), lambda i: (0, i, 0)),
        out_shape=jax.ShapeDtypeStruct((bsz, s, d_rnn), BF16),
        scratch_shapes=[
            pltpu.VMEM((d_rnn // LANES, bsz * (tt + SUBLANES), LANES), F32),
            pltpu.VMEM(((w_conv.shape[0] - 1) * bsz, d_rnn), F32),
            pltpu.VMEM((bsz, d_rnn), F32),
        ],
        compiler_params=_cparams(("arbitrary",)),
        name="rglru_a",
    )(x, g, w_in_xy, w_conv, b_conv, wbd, b_r, b_i, lam)


def _mixout_kernel(x_ref, main_ref, g_ref, wq_ref, km_ref, vm_ref, gq_ref, wout_ref, o_ref, *, n_heads, hd):
    x = x_ref[...]
    d_main = main_ref.shape[1]
    h = _rms(x, g_ref[...]).astype(BF16)
    qm = _dot(h, wq_ref[...])
    mo = _mem_attention(qm, km_ref, vm_ref, gq_ref[...], n_heads, hd).astype(BF16)
    y = _dot(main_ref[...], wout_ref[:d_main, :]) + _dot(mo, wout_ref[d_main:, :])
    o_ref[...] = x + y


def _mixout_call(x, main, g, w_q, km, vm, gq, w_out, n_heads, hd):
    bsz, s, d = x.shape
    d_main = main.shape[2]
    n_mem, mw = km.shape[1:]
    ts = TS_MIX
    const = lambda *shape: pl.BlockSpec(shape, lambda b, i: (0,) * len(shape))
    return pl.pallas_call(
        functools.partial(_mixout_kernel, n_heads=n_heads, hd=hd),
        grid=(bsz, s // ts),
        in_specs=[
            pl.BlockSpec((None, ts, d), lambda b, i: (b, i, 0)),
            pl.BlockSpec((None, ts, d_main), lambda b, i: (b, i, 0)),
            const(1, d),
            const(*w_q.shape),
            pl.BlockSpec((None, n_mem, mw), lambda b, i: (b, 0, 0)),
            pl.BlockSpec((None, n_mem, mw), lambda b, i: (b, 0, 0)),
            const(1, hd),
            const(*w_out.shape),
        ],
        out_specs=pl.BlockSpec((None, ts, d), lambda b, i: (b, i, 0)),
        out_shape=jax.ShapeDtypeStruct(x.shape, F32),
        compiler_params=_cparams(("arbitrary", "arbitrary")),
        name="mixout_a",
    )(x, main, g, w_q, km, vm, gq, w_out)


def _ffn_kernel(x_ref, g_ref, wu_ref, wg_ref, cu_ref, cg_ref, bu_ref, bg_ref, wd_ref, o_ref,
                pad_ref, h_ref, acc_ref, carry_ref):
    s_idx = pl.program_id(0)
    k = pl.program_id(1)
    nk = pl.num_programs(1)
    nb = x_ref.shape[0]

    @pl.when(k == 0)
    def _():
        h_ref[...] = _rms(_to_time_major(x_ref, pad_ref), g_ref[...]).astype(BF16)

    @pl.when(s_idx == 0)
    def _():
        carry_ref[k] = jnp.zeros(carry_ref.shape[1:], F32)

    hb = h_ref[...]
    n = hb.shape[0]
    keep = carry_ref.shape[2]

    def branch(w_ref, c_ref, b_ref, slot):
        pre = _dot(hb, w_ref[...])
        out = _causal_conv_tm(pre, carry_ref[k, slot], c_ref, b_ref, nb)
        carry_ref[k, slot] = pre[n - keep:, :]
        return out

    u = branch(wu_ref, cu_ref, bu_ref, 0)
    gt = branch(wg_ref, cg_ref, bg_ref, 1)
    act = (_gelu(gt) * u).astype(BF16)
    part = _dot(act, wd_ref[...])

    @pl.when(k == 0)
    def _():
        acc_ref[...] = part

    @pl.when(jnp.logical_and(k > 0, k < nk - 1))
    def _():
        acc_ref[...] += part

    @pl.when(k == nk - 1)
    def _():
        _from_time_major(acc_ref[...] + part, o_ref, pad_ref, res_ref=x_ref)


def _ffn_call(x, g, w_up, w_dw, b_dw, w_down):
    bsz, s, d = x.shape
    d_ff = w_down.shape[0]
    tt, fc = TT_FFN, FC_FFN
    nk = d_ff // fc
    assert nk >= 2
    width = w_dw.shape[0]
    return pl.pallas_call(
        _ffn_kernel,
        grid=(s // tt, nk),
        in_specs=[
            pl.BlockSpec((bsz, tt, d), lambda i, k: (0, i, 0)),
            pl.BlockSpec((1, d), lambda i, k: (0, 0)),
            pl.BlockSpec((d, fc), lambda i, k: (0, k)),
            pl.BlockSpec((d, fc), lambda i, k: (0, nk + k)),
            pl.BlockSpec((width, fc), lambda i, k: (0, k)),
            pl.BlockSpec((width, fc), lambda i, k: (0, nk + k)),
            pl.BlockSpec((1, fc), lambda i, k: (0, k)),
            pl.BlockSpec((1, fc), lambda i, k: (0, nk + k)),
            pl.BlockSpec((fc, d), lambda i, k: (k, 0)),
        ],
        out_specs=pl.BlockSpec((bsz, tt, d), lambda i, k: (0, i, 0)),
        out_shape=jax.ShapeDtypeStruct(x.shape, F32),
        scratch_shapes=[
            pltpu.VMEM((d // LANES, bsz * (tt + SUBLANES), LANES), F32),
            pltpu.VMEM((bsz * tt, d), BF16),
            pltpu.VMEM((bsz * tt, d), F32),
            pltpu.VMEM((nk, 2, (width - 1) * bsz, fc), F32),
        ],
        compiler_params=_cparams(("arbitrary", "arbitrary")),
        name="conv_ffn",
    )(x, g, w_up, w_up, w_dw, w_dw, b_dw, b_dw, w_down)


def _kv_kernel(x_ref, g_ref, wk_ref, wvt_ref, gk_ref, k_ref, vt_ref, *, hd):
    h = _rms(x_ref[...], g_ref[...]).astype(BF16)
    k = _dot(h, wk_ref[...])
    k_ref[...] = _head_rms_rows(k, gk_ref[...], hd).astype(BF16)
    vt = _dot_nt(wvt_ref[...], h)
    ext = hd + BF16_ROWS
    ones = jnp.ones((BF16_ROWS, vt.shape[1]), BF16)
    for hh in range(vt.shape[0] // hd):
        vt_ref[hh * ext:hh * ext + hd, :] = vt[hh * hd:(hh + 1) * hd, :].astype(BF16)
        vt_ref[hh * ext + hd:(hh + 1) * ext, :] = ones


def _kv_call(x, g, w_k, w_vt, gk_row, hd):
    bsz, s, d = x.shape
    w = w_k.shape[1]
    wx = (w // hd) * (hd + BF16_ROWS)
    ts = TS_KV
    return pl.pallas_call(
        functools.partial(_kv_kernel, hd=hd),
        grid=(bsz, s // ts),
        in_specs=[
            pl.BlockSpec((None, ts, d), lambda b, i: (b, i, 0)),
            pl.BlockSpec((1, d), lambda b, i: (0, 0)),
            pl.BlockSpec((d, w), lambda b, i: (0, 0)),
            pl.BlockSpec((w, d), lambda b, i: (0, 0)),
            pl.BlockSpec((1, w), lambda b, i: (0, 0)),
        ],
        out_specs=[
            pl.BlockSpec((None, ts, w), lambda b, i: (b, i, 0)),
            pl.BlockSpec((None, wx, ts), lambda b, i: (b, 0, i)),
        ],
        out_shape=[jax.ShapeDtypeStruct((bsz, s, w), BF16), jax.ShapeDtypeStruct((bsz, wx, s), BF16)],
        compiler_params=_cparams(("arbitrary", "arbitrary")),
        name="kv_proj",
    )(x, g, w_k, w_vt, gk_row)


def _bias_kernel(f_ref, o_ref):
    x = jnp.broadcast_to(f_ref[0], (CHUNK, KEY_ROWS))
    r = pltpu.roll(x, 0, 1, stride=1, stride_axis=0)
    band = r.T[CHUNK - 1:CHUNK - 1 + BAND, :]
    negs = jnp.full((CHUNK, CHUNK), NEG, F32)
    lo = jnp.concatenate([band, negs], axis=0)
    hi = jnp.concatenate([negs, band], axis=0)
    o_ref[0, 0:NEG_ROWS, :] = jnp.full((NEG_ROWS, 2 * CHUNK), NEG, F32)
    o_ref[0, NEG_ROWS:, :] = jnp.concatenate([lo, hi], axis=1)


def _bias_call(rel_bias):
    n_heads, n_rel = rel_bias.shape
    assert n_rel == MAX_REL + CHUNK and 2 * CHUNK == LANES
    f = jnp.concatenate([jnp.broadcast_to(rel_bias[:, n_rel - 1:], (n_heads, KEY_ROWS - 1 - n_rel)),
                         rel_bias[:, ::-1], rel_bias[:, :1]], axis=1) * LOG2E
    rows = NEG_ROWS + KEY_ROWS
    return pl.pallas_call(
        _bias_kernel,
        grid=(n_heads,),
        in_specs=[pl.BlockSpec((1, 1, KEY_ROWS), lambda h: (h, 0, 0))],
        out_specs=pl.BlockSpec((1, rows, LANES), lambda h: (h, 0, 0)),
        out_shape=jax.ShapeDtypeStruct((n_heads, rows, LANES), F32),
        compiler_params=_cparams(("arbitrary",)),
        name="band_bias",
    )(f.astype(F32)[:, None, :])


def _mixer_b_kernel(x_ref, g_ref, win_ref, gqa_ref, k0_ref, k1_ref, k2_ref, v0_ref, v1_ref, v2_ref,
                    pb_ref, km_ref, vm_ref, gq_ref, wout_ref, o_ref, qa_ref, s_ref, p_ref, att_ref,
                    *, att_w, att_hd, n_heads, hd):
    i = pl.program_id(1)
    x = x_ref[...]
    tq = x.shape[0]
    half = tq // 2
    assert half == LANES
    h = _rms(x, g_ref[...]).astype(BF16)
    proj = _dot(h, win_ref[...])
    qa_ref[...] = _head_rms_rows(proj[:, :att_w], gqa_ref[...], att_hd).astype(BF16)
    qm = proj[:, att_w:]

    k_refs = (k0_ref, k1_ref, k2_ref)
    v_refs = (v0_ref, v1_ref, v2_ref)
    ok0 = i >= 2
    ok1 = i >= 1
    lane = lax.broadcasted_iota(jnp.int32, (tq, LANES), 1)
    ext = att_hd + BF16_ROWS

    zblk = jnp.zeros((half, half), BF16)
    for slot in range(2):
        p_ref[slot, 0, 0:half, half:] = zblk
        p_ref[slot, 2, half:, 0:half] = zblk

    def bias_rows(hh, ok, start, size):
        first = pl.multiple_of(jnp.where(ok, NEG_ROWS + start, 0), LANES)
        return pb_ref[hh, pl.ds(first, size), :]

    def scores(hh):
        slot = hh % 2
        col = (hh // 2) * LANES
        q_pair = qa_ref[:, col:col + LANES]
        keep = (lane < att_hd) if hh % 2 == 0 else (lane >= att_hd)
        qz = jnp.where(keep, q_pair, jnp.zeros_like(q_pair))
        st = [_dot_nt(kr[:, col:col + LANES], qz) for kr in k_refs]
        s_ref[slot, 0, 0:tq, :] = st[0][:, :half] + bias_rows(hh, ok0, 0, tq)
        s_ref[slot, 0, tq:2 * tq, :] = st[1][:, :half] + bias_rows(hh, ok1, tq, tq)
        s_ref[slot, 0, 2 * tq:, :] = st[2][:half, :half] + pb_ref[hh, NEG_ROWS + 2 * tq:, :]
        s_ref[slot, 1, 0:half, :] = st[0][half:, half:] + bias_rows(hh, ok0, 0, half)
        s_ref[slot, 1, half:half + tq, :] = st[1][:, half:] + bias_rows(hh, ok1, half, tq)
        s_ref[slot, 1, half + tq:, :] = st[2][:, half:] + pb_ref[hh, NEG_ROWS + half + tq:, :]

    def probs(hh):
        slot = hh % 2
        z = s_ref[slot, 0]
        e = jnp.exp2(z - jnp.max(z, axis=0, keepdims=True)).astype(BF16)
        p_ref[slot, 0, :, 0:half] = e[0:tq]
        p_ref[slot, 1, :, 0:half] = e[tq:2 * tq]
        p_ref[slot, 2, 0:half, 0:half] = e[2 * tq:]
        z = s_ref[slot, 1]
        e = jnp.exp2(z - jnp.max(z, axis=0, keepdims=True)).astype(BF16)
        p_ref[slot, 0, half:, half:] = e[0:half]
        p_ref[slot, 1, :, half:] = e[half:half + tq]
        p_ref[slot, 2, :, half:] = e[half + tq:]

    def values(hh):
        slot = hh % 2
        ot = _dot(v_refs[0][hh * ext:(hh + 1) * ext, :], p_ref[slot, 0])
        ot = ot + _dot(v_refs[1][hh * ext:(hh + 1) * ext, :], p_ref[slot, 1])
        ot = ot + _dot(v_refs[2][hh * ext:(hh + 1) * ext, :], p_ref[slot, 2])
        att_ref[hh * att_hd:(hh + 1) * att_hd, :] = ot[:att_hd] / ot[att_hd:att_hd + 1]

    n_att = att_w // att_hd
    for step in range(n_att + 2):
        if 0 <= step - 2:
            values(step - 2)
        if 0 <= step - 1 < n_att:
            probs(step - 1)
        if step < n_att:
            scores(step)

    att = att_ref[...].T.astype(BF16)
    mo = _mem_attention(qm, km_ref, vm_ref, gq_ref[...], n_heads, hd).astype(BF16)
    y = _dot(att, wout_ref[:att_w, :]) + _dot(mo, wout_ref[att_w:, :])
    o_ref[...] = x + y


def _mixer_b_call(x, g, w_in, gqa_row, k, vt, pb, km, vm, gq, w_out, att_hd, n_heads, hd):
    bsz, s, d = x.shape
    att_w = k.shape[2]
    vt_w = vt.shape[1]
    n_mem, mw = km.shape[1:]
    tq = TQ
    const = lambda *shape: pl.BlockSpec(shape, lambda b, i: (0,) * len(shape))
    kspec = lambda r: pl.BlockSpec((None, tq, att_w), lambda b, i: (b, jnp.maximum(i - 2 + r, 0), 0))
    vspec = lambda r: pl.BlockSpec((None, vt_w, tq), lambda b, i: (b, 0, jnp.maximum(i - 2 + r, 0)))
    return pl.pallas_call(
        functools.partial(_mixer_b_kernel, att_w=att_w, att_hd=att_hd, n_heads=n_heads, hd=hd),
        grid=(bsz, s // tq),
        in_specs=[
            pl.BlockSpec((None, tq, d), lambda b, i: (b, i, 0)),
            const(1, d),
            const(*w_in.shape),
            const(1, att_w),
            kspec(0), kspec(1), kspec(2),
            vspec(0), vspec(1), vspec(2),
            const(*pb.shape),
            pl.BlockSpec((None, n_mem, mw), lambda b, i: (b, 0, 0)),
            pl.BlockSpec((None, n_mem, mw), lambda b, i: (b, 0, 0)),
            const(1, hd),
            const(*w_out.shape),
        ],
        out_specs=pl.BlockSpec((None, tq, d), lambda b, i: (b, i, 0)),
        out_shape=jax.ShapeDtypeStruct(x.shape, F32),
        scratch_shapes=[
            pltpu.VMEM((tq, att_w), BF16),
            pltpu.VMEM((2, 2, KEY_ROWS, LANES), F32),
            pltpu.VMEM((2, 3, tq, tq), BF16),
            pltpu.VMEM((att_w, tq), F32),
        ],
        compiler_params=_cparams(("arbitrary", "arbitrary")),
        name="mixer_b",
    )(x, g, w_in, gqa_row, k, k, k, vt, vt, vt, pb, km, vm, gq, w_out)


def _block_diag_halves(w_r, w_i):
    n, bw, _ = w_r.shape
    half_blocks = n // 2

    def dense_half(w, j):
        blocks = w[j * half_blocks:(j + 1) * half_blocks]
        eye = jnp.eye(half_blocks, dtype=w.dtype)
        return jnp.einsum('nkj,nm->nkmj', blocks, eye).reshape(half_blocks * bw, half_blocks * bw)

    halves = [jnp.concatenate([dense_half(w_r, j), dense_half(w_i, j)], axis=1) for j in range(2)]
    return jnp.stack(halves).astype(BF16)


def kernel(x, mem, g_mix, g_mem, w_mem_kv, g_q_mem, g_k_mem, w_in_a, w_conv_a, b_conv_a, w_r_a, b_r_a,
           w_i_a, b_i_a, lam_a, w_out_a, g_kv, w_kv, g_k_att, w_in_b, g_q_att, rel_bias_b, w_out_b,
           g_ffn, w_up, w_dw_ffn, b_dw_ffn, w_down):
    bsz, s, d = x.shape
    depth = g_mix.shape[0]
    n_a = w_in_a.shape[0]
    mem_hd = g_q_mem.shape[1]
    mem_w = w_mem_kv.shape[2] // 2
    mem_heads = mem_w // mem_hd
    att_hd = g_k_att.shape[0]
    att_w = w_kv.shape[1] // 2
    att_heads = att_w // att_hd
    d_rnn = w_conv_a.shape[2]
    assert att_hd * 2 == LANES and att_hd == CHUNK and bsz == SUBLANES
    assert s % TS_KV == 0 and s % TS_MIX == 0 and s % TQ == 0 and s % TT_FFN == 0 and s % TT_RGLRU == 0

    row = lambda v: v.reshape(1, -1).astype(F32)
    km_all, vm_all = _memkv_call(mem, g_mem, w_mem_kv, g_k_mem, mem_heads, mem_hd)

    k = vt = None
    for l in range(depth):
        if l < n_a:
            wbd = _block_diag_halves(w_r_a[l], w_i_a[l])
            w_in = w_in_a[l].astype(BF16)
            main = _rglru_call(x, row(g_mix[l]), w_in[:, :2 * d_rnn], w_conv_a[l], row(b_conv_a[l]), wbd,
                               row(b_r_a[l]), row(b_i_a[l]), row(lam_a[l]))
            x = _mixout_call(x, main, row(g_mix[l]), w_in[:, 2 * d_rnn:], km_all[l], vm_all[l],
                             row(g_q_mem[l]), w_out_a[l].astype(BF16), mem_heads, mem_hd)
        else:
            j = l - n_a
            if j == 0:
                k, vt = _kv_call(x, row(g_kv), w_kv[:, :att_w].astype(BF16), w_kv[:, att_w:].T.astype(BF16),
                                 row(jnp.tile(g_k_att, att_heads)), att_hd)
            gqa = row(jnp.tile(g_q_att[j], att_heads) * (att_hd ** -0.5 * LOG2E))
            x = _mixer_b_call(x, row(g_mix[l]), w_in_b[j].astype(BF16), gqa, k, vt, _bias_call(rel_bias_b[j]),
                              km_all[l], vm_all[l], row(g_q_mem[l]), w_out_b[j].astype(BF16),
                              att_hd, mem_heads, mem_hd)
        x = _ffn_call(x, row(g_ffn[l]), w_up[l].astype(BF16), w_dw_ffn[l], row(b_dw_ffn[l]),
                      w_down[l].astype(BF16))
    return x
```

```python
import functools
import math

import jax
import jax.numpy as jnp
from jax import lax
from jax.experimental import pallas as pl
from jax.experimental.pallas import tpu as pltpu

EPS = 1e-6
CHUNK = 64
N_PREV_CHUNKS = 8
BAND = (N_PREV_CHUNKS + 1) * CHUNK
MAX_REL = 256
RG_C = 8.0
NEG = -1e30
LOG2E = math.log2(math.e)

LANES = 128
SUBLANES = 8
BF16_ROWS = 16
VMEM_LIMIT = 56 * 1024 * 1024

TT_RGLRU = 32
TT_FFN = 64
FC_FFN = 1024
FFN_SUB = 256
TS_MIX = 512
TS_KV = 512
TQ = 4 * CHUNK
KEY_ROWS = BAND + CHUNK
NEG_ROWS = TQ

F32 = jnp.float32
BF16 = jnp.bfloat16


def _cparams(sem):
    return pltpu.CompilerParams(dimension_semantics=sem, vmem_limit_bytes=VMEM_LIMIT)


def _dot(a, b):
    return jnp.dot(a, b, preferred_element_type=F32)


def _dot_nt(a, b):
    return lax.dot_general(a, b, (((1,), (1,)), ((), ())), preferred_element_type=F32)


def _rms(x, g):
    ms = jnp.mean(x * x, axis=-1, keepdims=True)
    return x * lax.rsqrt(ms + EPS) * g


def _gelu(x):
    c = -2.0 * 0.7978845608028654 * LOG2E
    return x / (1.0 + jnp.exp2(x * (c + (c * 0.044715) * (x * x))))


def _to_time_major(x_ref, pad_ref):
    nb, tt, d = x_ref.shape
    pitch = pad_ref.shape[1] // nb
    for b in range(nb):
        for c in range(d // LANES):
            pad_ref[c, b * pitch:b * pitch + tt, :] = x_ref[b, :, c * LANES:(c + 1) * LANES].astype(F32)
    rows = []
    for t in range(tt):
        rows.append(jnp.concatenate(
            [pad_ref[c, pl.ds(t, nb, stride=pitch), :] for c in range(d // LANES)], axis=1))
    return jnp.concatenate(rows, axis=0)


def _from_time_major(y, o_ref, pad_ref, res_ref=None):
    nb, tt, d = o_ref.shape
    pitch = pad_ref.shape[1] // nb
    for t in range(tt):
        for c in range(d // LANES):
            pad_ref[c, pl.ds(t, nb, stride=pitch), :] = y[t * nb:(t + 1) * nb, c * LANES:(c + 1) * LANES]
    for b in range(nb):
        for c in range(d // LANES):
            v = pad_ref[c, b * pitch:b * pitch + tt, :]
            if res_ref is not None:
                v = v + res_ref[b, :, c * LANES:(c + 1) * LANES]
            o_ref[b, :, c * LANES:(c + 1) * LANES] = v.astype(o_ref.dtype)


def _causal_conv_tm(x, prev, w_ref, b_ref, nb, cols=slice(None)):
    n = x.shape[0]
    width = w_ref.shape[0]
    xc = jnp.concatenate([prev, x], axis=0)
    out = x * w_ref[width - 1:width, cols] + b_ref[:, cols]
    for k in range(width - 1):
        out = out + xc[k * nb:k * nb + n] * w_ref[k:k + 1, cols]
    return out


def _head_rms_rows(x, g, hd):
    t, w = x.shape
    lane = lax.broadcasted_iota(jnp.int32, (t, LANES), 1)
    lo = lane < hd
    outs = []
    for p in range(w // LANES):
        xp = x[:, p * LANES:(p + 1) * LANES]
        x2 = xp * xp
        s_lo = jnp.sum(jnp.where(lo, x2, 0.0), axis=-1, keepdims=True)
        s_hi = jnp.sum(jnp.where(lo, 0.0, x2), axis=-1, keepdims=True)
        r_lo = lax.rsqrt(s_lo * (1.0 / hd) + EPS)
        r_hi = lax.rsqrt(s_hi * (1.0 / hd) + EPS)
        outs.append(xp * jnp.where(lo, r_lo, r_hi))
    return jnp.concatenate(outs, axis=-1) * g


def _mem_attention(qm, km_ref, vm_ref, gq, n_heads, hd):
    outs = []
    for hh in range(n_heads):
        sl = slice(hh * hd, (hh + 1) * hd)
        qn = (_rms(qm[:, sl], gq) * (hd ** -0.5)).astype(BF16)
        s = _dot_nt(qn, km_ref[:, sl])
        m = jnp.max(s, axis=-1, keepdims=True)
        e = jnp.exp(s - m)
        l = jnp.sum(e, axis=-1, keepdims=True)
        o = _dot(e.astype(BF16), vm_ref[:, sl])
        outs.append(o / l)
    return jnp.concatenate(outs, axis=-1)


def _memkv_kernel(mem_ref, g_ref, w_ref, gk_ref, km_ref, vm_ref, *, n_heads, hd):
    mn = _rms(mem_ref[...], g_ref[...]).astype(BF16)
    kv = _dot(mn, w_ref[...])
    w = n_heads * hd
    ks = []
    for hh in range(n_heads):
        ks.append(_rms(kv[:, hh * hd:(hh + 1) * hd], gk_ref[...]))
    km_ref[...] = jnp.concatenate(ks, axis=-1).astype(BF16)
    vm_ref[...] = kv[:, w:].astype(BF16)


def _memkv_call(mem, g_mem, w_mem_kv, g_k_mem, n_heads, hd):
    depth = g_mem.shape[0]
    bsz, n_mem, d = mem.shape
    w = n_heads * hd
    out = jax.ShapeDtypeStruct((depth, bsz, n_mem, w), BF16)
    return pl.pallas_call(
        functools.partial(_memkv_kernel, n_heads=n_heads, hd=hd),
        grid=(depth, bsz),
        in_specs=[
            pl.BlockSpec((None, n_mem, d), lambda l, b: (b, 0, 0)),
            pl.BlockSpec((None, 1, d), lambda l, b: (l, 0, 0)),
            pl.BlockSpec((None, d, 2 * w), lambda l, b: (l, 0, 0)),
            pl.BlockSpec((None, 1, hd), lambda l, b: (l, 0, 0)),
        ],
        out_specs=[
            pl.BlockSpec((None, None, n_mem, w), lambda l, b: (l, b, 0, 0)),
            pl.BlockSpec((None, None, n_mem, w), lambda l, b: (l, b, 0, 0)),
        ],
        out_shape=[out, out],
        compiler_params=_cparams(("arbitrary", "arbitrary")),
        name="mem_kv",
    )(mem, g_mem.reshape(depth, 1, d), w_mem_kv.astype(BF16), g_k_mem.reshape(depth, 1, hd))


def _rglru_kernel(x_ref, g_ref, win_ref, wc_ref, bc_ref, wbd_ref, br_ref, bi_ref, lam_ref, o_ref,
                  pad_ref, xprev_ref, hprev_ref):
    @pl.when(pl.program_id(0) == 0)
    def _():
        xprev_ref[...] = jnp.zeros_like(xprev_ref)
        hprev_ref[...] = jnp.zeros_like(hprev_ref)

    nb, tt, _ = x_ref.shape
    d_rnn = wc_ref.shape[1]
    x = _to_time_major(x_ref, pad_ref)
    n = x.shape[0]
    h = _rms(x, g_ref[...]).astype(BF16)
    proj = _dot(h, win_ref[...])
    xr = proj[:, :d_rnn]
    yg = proj[:, d_rnn:]

    conv = _causal_conv_tm(xr, xprev_ref[...], wc_ref, bc_ref, nb)
    xprev_ref[...] = xr[n - xprev_ref.shape[0]:, :]

    half = d_rnn // 2
    cb = conv.astype(BF16)
    ri0 = _dot(cb[:, :half], wbd_ref[0])
    ri1 = _dot(cb[:, half:], wbd_ref[1])
    r = jax.nn.sigmoid(jnp.concatenate([ri0[:, :half], ri1[:, :half]], axis=-1) + br_ref[...])
    ig = jax.nn.sigmoid(jnp.concatenate([ri0[:, half:], ri1[:, half:]], axis=-1) + bi_ref[...])
    lam = lam_ref[...]
    log_sig = jnp.minimum(lam, 0.0) - jnp.log1p(jnp.exp(-jnp.abs(lam)))
    log_a = RG_C * r * log_sig
    a = jnp.exp(log_a)
    z = -jnp.tanh(log_a) * (a * a + 1.0)
    u = jnp.where(z > 0.0, z * lax.rsqrt(z), 0.0) * ig * conv

    hcur = hprev_ref[...]
    hs = []
    for t in range(tt):
        hcur = a[t * nb:(t + 1) * nb] * hcur + u[t * nb:(t + 1) * nb]
        hs.append(hcur)
    hprev_ref[...] = hcur
    main = jnp.concatenate(hs, axis=0) * _gelu(yg)
    _from_time_major(main, o_ref, pad_ref)


def _rglru_call(x, g, w_in_xy, w_conv, b_conv, wbd, b_r, b_i, lam):
    bsz, s, d = x.shape
    d_rnn = w_conv.shape[1]
    tt = TT_RGLRU
    const = lambda *shape: pl.BlockSpec(shape, lambda i: (0,) * len(shape))
    return pl.pallas_call(
        _rglru_kernel,
        grid=(s // tt,),
        in_specs=[
            pl.BlockSpec((bsz, tt, d), lambda i: (0, i, 0)),
            const(1, d),
            const(*w_in_xy.shape),
            const(*w_conv.shape),
            const(1, d_rnn),
            const(*wbd.shape),
            const(1, d_rnn),
            const(1, d_rnn),
            const(1, d_rnn),
        ],
        out_specs=pl.BlockSpec((bsz, tt, d_rnn), lambda i: (0, i, 0)),
        out_shape=jax.ShapeDtypeStruct((bsz, s, d_rnn), BF16),
        scratch_shapes=[
            pltpu.VMEM((d_rnn // LANES, bsz * (tt + SUBLANES), LANES), F32),
            pltpu.VMEM(((w_conv.shape[0] - 1) * bsz, d_rnn), F32),
            pltpu.VMEM((bsz, d_rnn), F32),
        ],
        compiler_params=_cparams(("arbitrary",)),
        name="rglru_a",
    )(x, g, w_in_xy, w_conv, b_conv, wbd, b_r, b_i, lam)


def _mixout_kernel(x_ref, main_ref, g_ref, wq_ref, km_ref, vm_ref, gq_ref, wout_ref, o_ref, *, n_heads, hd):
    x = x_ref[...]
    d_main = main_ref.shape[1]
    h = _rms(x, g_ref[...]).astype(BF16)
    qm = _dot(h, wq_ref[...])
    mo = _mem_attention(qm, km_ref, vm_ref, gq_ref[...], n_heads, hd).astype(BF16)
    y = _dot(main_ref[...], wout_ref[:d_main, :]) + _dot(mo, wout_ref[d_main:, :])
    o_ref[...] = x + y


def _mixout_call(x, main, g, w_q, km, vm, gq, w_out, n_heads, hd):
    bsz, s, d = x.shape
    d_main = main.shape[2]
    n_mem, mw = km.shape[1:]
    ts = TS_MIX
    const = lambda *shape: pl.BlockSpec(shape, lambda b, i: (0,) * len(shape))
    return pl.pallas_call(
        functools.partial(_mixout_kernel, n_heads=n_heads, hd=hd),
        grid=(bsz, s // ts),
        in_specs=[
            pl.BlockSpec((None, ts, d), lambda b, i: (b, i, 0)),
            pl.BlockSpec((None, ts, d_main), lambda b, i: (b, i, 0)),
            const(1, d),
            const(*w_q.shape),
            pl.BlockSpec((None, n_mem, mw), lambda b, i: (b, 0, 0)),
            pl.BlockSpec((None, n_mem, mw), lambda b, i: (b, 0, 0)),
            const(1, hd),
            const(*w_out.shape),
        ],
        out_specs=pl.BlockSpec((None, ts, d), lambda b, i: (b, i, 0)),
        out_shape=jax.ShapeDtypeStruct(x.shape, F32),
        compiler_params=_cparams(("arbitrary", "arbitrary")),
        name="mixout_a",
    )(x, main, g, w_q, km, vm, gq, w_out)


def _ffn_kernel(x_ref, g_ref, wu_ref, wg_ref, cu_ref, cg_ref, bu_ref, bg_ref, wd_ref, o_ref,
                pad_ref, h_ref, acc_ref, carry_ref):
    s_idx = pl.program_id(0)
    k = pl.program_id(1)
    nk = pl.num_programs(1)
    nb = x_ref.shape[0]

    @pl.when(k == 0)
    def _():
        h_ref[...] = pltpu.bitcast(_rms(_to_time_major(x_ref, pad_ref), g_ref[...]).astype(BF16), jnp.uint32)
        acc_ref[...] = jnp.zeros_like(acc_ref)

    @pl.when(s_idx == 0)
    def _():
        carry_ref[k] = jnp.zeros(carry_ref.shape[1:], F32)

    hb = pltpu.bitcast(h_ref[...], BF16)
    n = hb.shape[0]
    keep = carry_ref.shape[2]

    def branch(w_ref, c_ref, b_ref, slot, cols):
        pre = _dot(hb, w_ref[:, cols])
        out = _causal_conv_tm(pre, carry_ref[k, slot, :, cols], c_ref, b_ref, nb, cols)
        carry_ref[k, slot, :, cols] = pre[n - keep:, :]
        return out

    acts = []
    for j in range(wu_ref.shape[1] // FFN_SUB):
        cols = slice(j * FFN_SUB, (j + 1) * FFN_SUB)
        u = branch(wu_ref, cu_ref, bu_ref, 0, cols)
        gt = branch(wg_ref, cg_ref, bg_ref, 1, cols)
        acts.append((_gelu(gt) * u).astype(BF16))
    act = jnp.concatenate(acts, axis=1)
    for j in range(wd_ref.shape[1] // FFN_SUB):
        cols = slice(j * FFN_SUB, (j + 1) * FFN_SUB)
        acc_ref[:, cols] += _dot(act, wd_ref[:, cols])

    @pl.when(k == nk - 1)
    def _():
        _from_time_major(acc_ref[...], o_ref, pad_ref, res_ref=x_ref)


def _ffn_call(x, g, w_up, w_dw, b_dw, w_down):
    bsz, s, d = x.shape
    d_ff = w_down.shape[0]
    tt, fc = TT_FFN, FC_FFN
    nk = d_ff // fc
    assert nk >= 2
    width = w_dw.shape[0]
    return pl.pallas_call(
        _ffn_kernel,
        grid=(s // tt, nk),
        in_specs=[
            pl.BlockSpec((bsz, tt, d), lambda i, k: (0, i, 0)),
            pl.BlockSpec((1, d), lambda i, k: (0, 0)),
            pl.BlockSpec((d, fc), lambda i, k: (0, k)),
            pl.BlockSpec((d, fc), lambda i, k: (0, nk + k)),
            pl.BlockSpec((width, fc), lambda i, k: (0, k)),
            pl.BlockSpec((width, fc), lambda i, k: (0, nk + k)),
            pl.BlockSpec((1, fc), lambda i, k: (0, k)),
            pl.BlockSpec((1, fc), lambda i, k: (0, nk + k)),
            pl.BlockSpec((fc, d), lambda i, k: (k, 0)),
        ],
        out_specs=pl.BlockSpec((bsz, tt, d), lambda i, k: (0, i, 0)),
        out_shape=jax.ShapeDtypeStruct(x.shape, F32),
        scratch_shapes=[
            pltpu.VMEM((d // LANES, bsz * (tt + SUBLANES), LANES), F32),
            pltpu.VMEM((bsz * tt // 2, d), jnp.uint32),
            pltpu.VMEM((bsz * tt, d), F32),
            pltpu.VMEM((nk, 2, (width - 1) * bsz, fc), F32),
        ],
        compiler_params=_cparams(("arbitrary", "arbitrary")),
        name="conv_ffn",
    )(x, g, w_up, w_up, w_dw, w_dw, b_dw, b_dw, w_down)


def _kv_kernel(x_ref, g_ref, wk_ref, wvt_ref, gk_ref, k_ref, vt_ref, *, hd):
    h = _rms(x_ref[...], g_ref[...]).astype(BF16)
    k = _dot(h, wk_ref[...])
    k_ref[...] = _head_rms_rows(k, gk_ref[...], hd).astype(BF16)
    vt = _dot_nt(wvt_ref[...], h)
    ext = hd + BF16_ROWS
    n_blk, _, tq = vt_ref.shape
    ones = jnp.ones((BF16_ROWS, tq), BF16)
    for j in range(n_blk):
        for hh in range(vt.shape[0] // hd):
            vt_ref[j, hh * ext:hh * ext + hd, :] = vt[hh * hd:(hh + 1) * hd, j * tq:(j + 1) * tq].astype(BF16)
            vt_ref[j, hh * ext + hd:(hh + 1) * ext, :] = ones


def _kv_call(x, g, w_k, w_vt, gk_row, hd):
    bsz, s, d = x.shape
    w = w_k.shape[1]
    wx = (w // hd) * (hd + BF16_ROWS)
    ts = TS_KV
    return pl.pallas_call(
        functools.partial(_kv_kernel, hd=hd),
        grid=(bsz, s // ts),
        in_specs=[
            pl.BlockSpec((None, ts, d), lambda b, i: (b, i, 0)),
            pl.BlockSpec((1, d), lambda b, i: (0, 0)),
            pl.BlockSpec((d, w), lambda b, i: (0, 0)),
            pl.BlockSpec((w, d), lambda b, i: (0, 0)),
            pl.BlockSpec((1, w), lambda b, i: (0, 0)),
        ],
        out_specs=[
            pl.BlockSpec((None, ts, w), lambda b, i: (b, i, 0)),
            pl.BlockSpec((None, ts // TQ, wx, TQ), lambda b, i: (b, i, 0, 0)),
        ],
        out_shape=[jax.ShapeDtypeStruct((bsz, s, w), BF16),
                   jax.ShapeDtypeStruct((bsz, s // TQ, wx, TQ), BF16)],
        compiler_params=_cparams(("arbitrary", "arbitrary")),
        name="kv_proj",
    )(x, g, w_k, w_vt, gk_row)


def _bias_kernel(f_ref, o_ref):
    x = jnp.broadcast_to(f_ref[0], (CHUNK, KEY_ROWS))
    r = pltpu.roll(x, 0, 1, stride=1, stride_axis=0)
    band = r.T[CHUNK - 1:CHUNK - 1 + BAND, :]
    negs = jnp.full((CHUNK, CHUNK), NEG, F32)
    lo = jnp.concatenate([band, negs], axis=0)
    hi = jnp.concatenate([negs, band], axis=0)
    o_ref[0, 0:NEG_ROWS, :] = jnp.full((NEG_ROWS, 2 * CHUNK), NEG, F32)
    o_ref[0, NEG_ROWS:, :] = jnp.concatenate([lo, hi], axis=1)


def _bias_call(rel_bias):
    n_heads, n_rel = rel_bias.shape
    assert n_rel == MAX_REL + CHUNK and 2 * CHUNK == LANES
    f = jnp.concatenate([jnp.broadcast_to(rel_bias[:, n_rel - 1:], (n_heads, KEY_ROWS - 1 - n_rel)),
                         rel_bias[:, ::-1], rel_bias[:, :1]], axis=1) * LOG2E
    rows = NEG_ROWS + KEY_ROWS
    return pl.pallas_call(
        _bias_kernel,
        grid=(n_heads,),
        in_specs=[pl.BlockSpec((1, 1, KEY_ROWS), lambda h: (h, 0, 0))],
        out_specs=pl.BlockSpec((1, rows, LANES), lambda h: (h, 0, 0)),
        out_shape=jax.ShapeDtypeStruct((n_heads, rows, LANES), F32),
        compiler_params=_cparams(("arbitrary",)),
        name="band_bias",
    )(f.astype(F32)[:, None, :])


def _mixer_b_kernel(x_ref, g_ref, win_ref, gqa_ref, k0_ref, k1_ref, k2_ref, v0_ref, v1_ref, v2_ref,
                    pb_ref, km_ref, vm_ref, gq_ref, wout_ref, o_ref, qa_ref, s_ref, p_ref, att_ref,
                    *, att_w, att_hd, n_heads, hd):
    i = pl.program_id(1)
    x = x_ref[...]
    tq = x.shape[0]
    half = tq // 2
    assert half == LANES
    h = _rms(x, g_ref[...]).astype(BF16)
    proj = _dot(h, win_ref[...])
    qa_ref[...] = pltpu.bitcast(_head_rms_rows(proj[:, :att_w], gqa_ref[...], att_hd).astype(BF16), jnp.uint32)
    qm = proj[:, att_w:]

    k_refs = (k0_ref, k1_ref, k2_ref)
    v_refs = (v0_ref, v1_ref, v2_ref)
    ok0 = i >= 2
    ok1 = i >= 1
    lane = lax.broadcasted_iota(jnp.int32, (tq, LANES), 1)
    ext = att_hd + BF16_ROWS

    zblk = jnp.zeros((half, half), BF16)
    for slot in range(2):
        p_ref[slot, 0, 0:half, half:] = zblk
        p_ref[slot, 2, half:, 0:half] = zblk

    def bias_rows(hh, ok, start, size):
        first = pl.multiple_of(jnp.where(ok, NEG_ROWS + start, 0), LANES)
        return pb_ref[hh, pl.ds(first, size), :]

    def scores(hh):
        slot = hh % 2
        col = (hh // 2) * LANES
        q_pair = pltpu.bitcast(qa_ref[:, col:col + LANES], BF16)
        keep = (lane < att_hd) if hh % 2 == 0 else (lane >= att_hd)
        qz = jnp.where(keep, q_pair, jnp.zeros_like(q_pair))
        st = [_dot_nt(kr[:, col:col + LANES], qz) for kr in k_refs]
        s_ref[slot, 0, 0:tq, :] = st[0][:, :half] + bias_rows(hh, ok0, 0, tq)
        s_ref[slot, 0, tq:2 * tq, :] = st[1][:, :half] + bias_rows(hh, ok1, tq, tq)
        s_ref[slot, 0, 2 * tq:, :] = st[2][:half, :half] + pb_ref[hh, NEG_ROWS + 2 * tq:, :]
        s_ref[slot, 1, 0:half, :] = st[0][half:, half:] + bias_rows(hh, ok0, 0, half)
        s_ref[slot, 1, half:half + tq, :] = st[1][:, half:] + bias_rows(hh, ok1, half, tq)
        s_ref[slot, 1, half + tq:, :] = st[2][:, half:] + pb_ref[hh, NEG_ROWS + half + tq:, :]

    def probs(hh):
        slot = hh % 2
        z = s_ref[slot, 0]
        e = jnp.exp2(z - jnp.max(z, axis=0, keepdims=True)).astype(BF16)
        p_ref[slot, 0, :, 0:half] = e[0:tq]
        p_ref[slot, 1, :, 0:half] = e[tq:2 * tq]
        p_ref[slot, 2, 0:half, 0:half] = e[2 * tq:]
        z = s_ref[slot, 1]
        e = jnp.exp2(z - jnp.max(z, axis=0, keepdims=True)).astype(BF16)
        p_ref[slot, 0, half:, half:] = e[0:half]
        p_ref[slot, 1, :, half:] = e[half:half + tq]
        p_ref[slot, 2, :, half:] = e[half + tq:]

    def values(hh):
        slot = hh % 2
        ot = _dot(v_refs[0][hh * ext:(hh + 1) * ext, :], p_ref[slot, 0])
        ot = ot + _dot(v_refs[1][hh * ext:(hh + 1) * ext, :], p_ref[slot, 1])
        ot = ot + _dot(v_refs[2][hh * ext:(hh + 1) * ext, :], p_ref[slot, 2])
        att_ref[hh * att_hd:(hh + 1) * att_hd, :] = ot[:att_hd] / ot[att_hd:att_hd + 1]

    n_att = att_w // att_hd
    for step in range(n_att + 2):
        if 0 <= step - 2:
            values(step - 2)
        if 0 <= step - 1 < n_att:
            probs(step - 1)
        if step < n_att:
            scores(step)

    att = att_ref[...].T.astype(BF16)
    mo = _mem_attention(qm, km_ref, vm_ref, gq_ref[...], n_heads, hd).astype(BF16)
    y = _dot(att, wout_ref[:att_w, :]) + _dot(mo, wout_ref[att_w:, :])
    o_ref[...] = x + y


def _mixer_b_call(x, g, w_in, gqa_row, k, vt, pb, km, vm, gq, w_out, att_hd, n_heads, hd):
    bsz, s, d = x.shape
    att_w = k.shape[2]
    vt_w = vt.shape[2]
    n_mem, mw = km.shape[1:]
    tq = TQ
    const = lambda *shape: pl.BlockSpec(shape, lambda b, i: (0,) * len(shape))
    kspec = lambda r: pl.BlockSpec((None, tq, att_w), lambda b, i: (b, jnp.maximum(i - 2 + r, 0), 0))
    vspec = lambda r: pl.BlockSpec((None, None, vt_w, tq), lambda b, i: (b, jnp.maximum(i - 2 + r, 0), 0, 0))
    return pl.pallas_call(
        functools.partial(_mixer_b_kernel, att_w=att_w, att_hd=att_hd, n_heads=n_heads, hd=hd),
        grid=(bsz, s // tq),
        in_specs=[
            pl.BlockSpec((None, tq, d), lambda b, i: (b, i, 0)),
            const(1, d),
            const(*w_in.shape),
            const(1, att_w),
            kspec(0), kspec(1), kspec(2),
            vspec(0), vspec(1), vspec(2),
            const(*pb.shape),
            pl.BlockSpec((None, n_mem, mw), lambda b, i: (b, 0, 0)),
            pl.BlockSpec((None, n_mem, mw), lambda b, i: (b, 0, 0)),
            const(1, hd),
            const(*w_out.shape),
        ],
        out_specs=pl.BlockSpec((None, tq, d), lambda b, i: (b, i, 0)),
        out_shape=jax.ShapeDtypeStruct(x.shape, F32),
        scratch_shapes=[
            pltpu.VMEM((tq // 2, att_w), jnp.uint32),
            pltpu.VMEM((2, 2, KEY_ROWS, LANES), F32),
            pltpu.VMEM((2, 3, tq, tq), BF16),
            pltpu.VMEM((att_w, tq), F32),
        ],
        compiler_params=_cparams(("arbitrary", "arbitrary")),
        name="mixer_b",
    )(x, g, w_in, gqa_row, k, k, k, vt, vt, vt, pb, km, vm, gq, w_out)


def _block_diag_halves(w_r, w_i):
    n, bw, _ = w_r.shape
    half_blocks = n // 2

    def dense_half(w, j):
        blocks = w[j * half_blocks:(j + 1) * half_blocks]
        eye = jnp.eye(half_blocks, dtype=w.dtype)
        return jnp.einsum('nkj,nm->nkmj', blocks, eye).reshape(half_blocks * bw, half_blocks * bw)

    halves = [jnp.concatenate([dense_half(w_r, j), dense_half(w_i, j)], axis=1) for j in range(2)]
    return jnp.stack(halves).astype(BF16)


def kernel(x, mem, g_mix, g_mem, w_mem_kv, g_q_mem, g_k_mem, w_in_a, w_conv_a, b_conv_a, w_r_a, b_r_a,
           w_i_a, b_i_a, lam_a, w_out_a, g_kv, w_kv, g_k_att, w_in_b, g_q_att, rel_bias_b, w_out_b,
           g_ffn, w_up, w_dw_ffn, b_dw_ffn, w_down):
    bsz, s, d = x.shape
    depth = g_mix.shape[0]
    n_a = w_in_a.shape[0]
    mem_hd = g_q_mem.shape[1]
    mem_w = w_mem_kv.shape[2] // 2
    mem_heads = mem_w // mem_hd
    att_hd = g_k_att.shape[0]
    att_w = w_kv.shape[1] // 2
    att_heads = att_w // att_hd
    d_rnn = w_conv_a.shape[2]
    assert att_hd * 2 == LANES and att_hd == CHUNK and bsz == SUBLANES
    assert s % TS_KV == 0 and s % TS_MIX == 0 and s % TQ == 0 and s % TT_FFN == 0 and s % TT_RGLRU == 0

    row = lambda v: v.reshape(1, -1).astype(F32)
    km_all, vm_all = _memkv_call(mem, g_mem, w_mem_kv, g_k_mem, mem_heads, mem_hd)

    k = vt = None
    for l in range(depth):
        if l < n_a:
            wbd = _block_diag_halves(w_r_a[l], w_i_a[l])
            w_in = w_in_a[l].astype(BF16)
            main = _rglru_call(x, row(g_mix[l]), w_in[:, :2 * d_rnn], w_conv_a[l], row(b_conv_a[l]), wbd,
                               row(b_r_a[l]), row(b_i_a[l]), row(lam_a[l]))
            x = _mixout_call(x, main, row(g_mix[l]), w_in[:, 2 * d_rnn:], km_all[l], vm_all[l],
                             row(g_q_mem[l]), w_out_a[l].astype(BF16), mem_heads, mem_hd)
        else:
            j = l - n_a
            if j == 0:
                k, vt = _kv_call(x, row(g_kv), w_kv[:, :att_w].astype(BF16), w_kv[:, att_w:].T.astype(BF16),
                                 row(jnp.tile(g_k_att, att_heads)), att_hd)
            gqa = row(jnp.tile(g_q_att[j], att_heads) * (att_hd ** -0.5 * LOG2E))
            x = _mixer_b_call(x, row(g_mix[l]), w_in_b[j].astype(BF16), gqa, k, vt, _bias_call(rel_bias_b[j]),
                              km_all[l], vm_all[l], row(g_q_mem[l]), w_out_b[j].astype(BF16),
                              att_hd, mem_heads, mem_hd)
        x = _ffn_call(x, row(g_ffn[l]), w_up[l].astype(BF16), w_dw_ffn[l], row(b_dw_ffn[l]),
                      w_down[l].astype(BF16))
    return x
```

```python
import functools
import math

import jax
import jax.numpy as jnp
from jax import lax
from jax.experimental import pallas as pl
from jax.experimental.pallas import tpu as pltpu

EPS = 1e-6
CHUNK = 64
N_PREV_CHUNKS = 8
BAND = (N_PREV_CHUNKS + 1) * CHUNK
MAX_REL = 256
RG_C = 8.0
NEG = -1e30
LOG2E = math.log2(math.e)

LANES = 128
SUBLANES = 8
BF16_ROWS = 16
MXU_COLS = 256
BD_WINDOW = 512
VMEM_LIMIT = 56 * 1024 * 1024

TT_RGLRU = 32
TT_FFN = 64
FFN_SUB = 256
TS_MIX = 512
TS_KV = 512
TQ = 4 * CHUNK
KEY_ROWS = BAND + CHUNK
NEG_ROWS = TQ

F32 = jnp.float32
BF16 = jnp.bfloat16


def _cparams(sem):
    return pltpu.CompilerParams(dimension_semantics=sem, vmem_limit_bytes=VMEM_LIMIT)


def _dot(a, b):
    return jnp.dot(a, b, preferred_element_type=F32)


def _dot_nt(a, b):
    return lax.dot_general(a, b, (((1,), (1,)), ((), ())), preferred_element_type=F32)


def _rms(x, g):
    ms = jnp.mean(x * x, axis=-1, keepdims=True)
    return x * lax.rsqrt(ms + EPS) * g


def _gelu(x):
    c = -2.0 * 0.7978845608028654 * LOG2E
    return x / (1.0 + jnp.exp2(x * (c + (c * 0.044715) * (x * x))))


def _to_time_major(x_ref, pad_ref):
    nb, tt, d = x_ref.shape
    pitch = pad_ref.shape[1] // nb
    for b in range(nb):
        for c in range(d // LANES):
            pad_ref[c, b * pitch:b * pitch + tt, :] = x_ref[b, :, c * LANES:(c + 1) * LANES].astype(F32)
    rows = []
    for t in range(tt):
        rows.append(jnp.concatenate(
            [pad_ref[c, pl.ds(t, nb, stride=pitch), :] for c in range(d // LANES)], axis=1))
    return jnp.concatenate(rows, axis=0)


def _from_time_major(y, o_ref, pad_ref, res_ref=None):
    nb, tt, d = o_ref.shape
    pitch = pad_ref.shape[1] // nb
    for t in range(tt):
        for c in range(d // LANES):
            pad_ref[c, pl.ds(t, nb, stride=pitch), :] = y[t * nb:(t + 1) * nb, c * LANES:(c + 1) * LANES]
    for b in range(nb):
        for c in range(d // LANES):
            v = pad_ref[c, b * pitch:b * pitch + tt, :]
            if res_ref is not None:
                v = v + res_ref[b, :, c * LANES:(c + 1) * LANES]
            o_ref[b, :, c * LANES:(c + 1) * LANES] = v.astype(o_ref.dtype)


def _causal_conv_tm(x, prev, w_ref, b_ref, nb, cols=slice(None)):
    n = x.shape[0]
    width = w_ref.shape[0]
    xc = jnp.concatenate([prev, x], axis=0)
    out = x * w_ref[width - 1:width, cols] + b_ref[:, cols]
    for k in range(width - 1):
        out = out + xc[k * nb:k * nb + n] * w_ref[k:k + 1, cols]
    return out


def _head_rms_rows(x, g, hd):
    t, w = x.shape
    lane = lax.broadcasted_iota(jnp.int32, (t, LANES), 1)
    lo = lane < hd
    outs = []
    for p in range(w // LANES):
        xp = x[:, p * LANES:(p + 1) * LANES]
        x2 = xp * xp
        s_lo = jnp.sum(jnp.where(lo, x2, 0.0), axis=-1, keepdims=True)
        s_hi = jnp.sum(jnp.where(lo, 0.0, x2), axis=-1, keepdims=True)
        r_lo = lax.rsqrt(s_lo * (1.0 / hd) + EPS)
        r_hi = lax.rsqrt(s_hi * (1.0 / hd) + EPS)
        outs.append(xp * jnp.where(lo, r_lo, r_hi))
    return jnp.concatenate(outs, axis=-1) * g


def _mem_attention(qm, km_ref, vm_ref, gq, n_heads, hd):
    outs = []
    for hh in range(n_heads):
        sl = slice(hh * hd, (hh + 1) * hd)
        qn = (_rms(qm[:, sl], gq) * (hd ** -0.5)).astype(BF16)
        s = _dot_nt(qn, km_ref[:, sl])
        m = jnp.max(s, axis=-1, keepdims=True)
        e = jnp.exp(s - m)
        l = jnp.sum(e, axis=-1, keepdims=True)
        o = _dot(e.astype(BF16), vm_ref[:, sl])
        outs.append(o / l)
    return jnp.concatenate(outs, axis=-1)


def _memkv_kernel(mem_ref, g_ref, w_ref, gk_ref, km_ref, vm_ref, *, n_heads, hd):
    mn = _rms(mem_ref[...], g_ref[...]).astype(BF16)
    kv = _dot(mn, w_ref[...])
    w = n_heads * hd
    ks = []
    for hh in range(n_heads):
        ks.append(_rms(kv[:, hh * hd:(hh + 1) * hd], gk_ref[...]))
    km_ref[...] = jnp.concatenate(ks, axis=-1).astype(BF16)
    vm_ref[...] = kv[:, w:].astype(BF16)


def _memkv_call(mem, g_mem, w_mem_kv, g_k_mem, n_heads, hd):
    depth = g_mem.shape[0]
    bsz, n_mem, d = mem.shape
    w = n_heads * hd
    out = jax.ShapeDtypeStruct((depth, bsz, n_mem, w), BF16)
    return pl.pallas_call(
        functools.partial(_memkv_kernel, n_heads=n_heads, hd=hd),
        grid=(depth, bsz),
        in_specs=[
            pl.BlockSpec((None, n_mem, d), lambda l, b: (b, 0, 0)),
            pl.BlockSpec((None, 1, d), lambda l, b: (l, 0, 0)),
            pl.BlockSpec((None, d, 2 * w), lambda l, b: (l, 0, 0)),
            pl.BlockSpec((None, 1, hd), lambda l, b: (l, 0, 0)),
        ],
        out_specs=[
            pl.BlockSpec((None, None, n_mem, w), lambda l, b: (l, b, 0, 0)),
            pl.BlockSpec((None, None, n_mem, w), lambda l, b: (l, b, 0, 0)),
        ],
        out_shape=[out, out],
        compiler_params=_cparams(("arbitrary", "arbitrary")),
        name="mem_kv",
    )(mem, g_mem.reshape(depth, 1, d), w_mem_kv.astype(BF16), g_k_mem.reshape(depth, 1, hd))


def _rglru_kernel(x_ref, g_ref, win_ref, wc_ref, bc_ref, wbd_ref, br_ref, bi_ref, lam_ref, o_ref,
                  pad_ref, xprev_ref, hprev_ref, *, bd_starts):
    @pl.when(pl.program_id(0) == 0)
    def _():
        xprev_ref[...] = jnp.zeros_like(xprev_ref)
        hprev_ref[...] = jnp.zeros_like(hprev_ref)

    nb, tt, _ = x_ref.shape
    d_rnn = wc_ref.shape[1]
    x = _to_time_major(x_ref, pad_ref)
    n = x.shape[0]
    h = _rms(x, g_ref[...]).astype(BF16)
    proj = _dot(h, win_ref[...])
    xr = proj[:, :d_rnn]
    yg = proj[:, d_rnn:]

    conv = _causal_conv_tm(xr, xprev_ref[...], wc_ref, bc_ref, nb)
    xprev_ref[...] = xr[n - xprev_ref.shape[0]:, :]

    cb = conv.astype(BF16)
    win = wbd_ref.shape[1]
    ris = [_dot(cb[:, ws:ws + win], wbd_ref[j]) for j, ws in enumerate(bd_starts)]
    r = jax.nn.sigmoid(jnp.concatenate([ri[:, :MXU_COLS] for ri in ris], axis=-1) + br_ref[...])
    ig = jax.nn.sigmoid(jnp.concatenate([ri[:, MXU_COLS:] for ri in ris], axis=-1) + bi_ref[...])
    lam = lam_ref[...]
    log_sig = jnp.minimum(lam, 0.0) - jnp.log1p(jnp.exp(-jnp.abs(lam)))
    log_a = RG_C * r * log_sig
    a = jnp.exp(log_a)
    z = -jnp.tanh(log_a) * (a * a + 1.0)
    u = jnp.where(z > 0.0, z * lax.rsqrt(z), 0.0) * ig * conv

    hcur = hprev_ref[...]
    hs = []
    for t in range(tt):
        hcur = a[t * nb:(t + 1) * nb] * hcur + u[t * nb:(t + 1) * nb]
        hs.append(hcur)
    hprev_ref[...] = hcur
    main = jnp.concatenate(hs, axis=0) * _gelu(yg)
    _from_time_major(main, o_ref, pad_ref)


def _rglru_call(x, g, w_in_xy, w_conv, b_conv, wbd, bd_starts, b_r, b_i, lam):
    bsz, s, d = x.shape
    d_rnn = w_conv.shape[1]
    tt = TT_RGLRU
    const = lambda *shape: pl.BlockSpec(shape, lambda i: (0,) * len(shape))
    return pl.pallas_call(
        functools.partial(_rglru_kernel, bd_starts=bd_starts),
        grid=(s // tt,),
        in_specs=[
            pl.BlockSpec((bsz, tt, d), lambda i: (0, i, 0)),
            const(1, d),
            const(*w_in_xy.shape),
            const(*w_conv.shape),
            const(1, d_rnn),
            const(*wbd.shape),
            const(1, d_rnn),
            const(1, d_rnn),
            const(1, d_rnn),
        ],
        out_specs=pl.BlockSpec((bsz, tt, d_rnn), lambda i: (0, i, 0)),
        out_shape=jax.ShapeDtypeStruct((bsz, s, d_rnn), BF16),
        scratch_shapes=[
            pltpu.VMEM((d_rnn // LANES, bsz * (tt + SUBLANES), LANES), F32),
            pltpu.VMEM(((w_conv.shape[0] - 1) * bsz, d_rnn), F32),
            pltpu.VMEM((bsz, d_rnn), F32),
        ],
        compiler_params=_cparams(("arbitrary",)),
        name="rglru_a",
    )(x, g, w_in_xy, w_conv, b_conv, wbd, b_r, b_i, lam)


def _mixout_kernel(x_ref, main_ref, g_ref, wq_ref, km_ref, vm_ref, gq_ref, wout_ref, o_ref, *, n_heads, hd):
    x = x_ref[...]
    d_main = main_ref.shape[1]
    h = _rms(x, g_ref[...]).astype(BF16)
    qm = _dot(h, wq_ref[...])
    mo = _mem_attention(qm, km_ref, vm_ref, gq_ref[...], n_heads, hd).astype(BF16)
    y = _dot(main_ref[...], wout_ref[:d_main, :]) + _dot(mo, wout_ref[d_main:, :])
    o_ref[...] = x + y


def _mixout_call(x, main, g, w_q, km, vm, gq, w_out, n_heads, hd):
    bsz, s, d = x.shape
    d_main = main.shape[2]
    n_mem, mw = km.shape[1:]
    ts = TS_MIX
    const = lambda *shape: pl.BlockSpec(shape, lambda b, i: (0,) * len(shape))
    return pl.pallas_call(
        functools.partial(_mixout_kernel, n_heads=n_heads, hd=hd),
        grid=(bsz, s // ts),
        in_specs=[
            pl.BlockSpec((None, ts, d), lambda b, i: (b, i, 0)),
            pl.BlockSpec((None, ts, d_main), lambda b, i: (b, i, 0)),
            const(1, d),
            const(*w_q.shape),
            pl.BlockSpec((None, n_mem, mw), lambda b, i: (b, 0, 0)),
            pl.BlockSpec((None, n_mem, mw), lambda b, i: (b, 0, 0)),
            const(1, hd),
            const(*w_out.shape),
        ],
        out_specs=pl.BlockSpec((None, ts, d), lambda b, i: (b, i, 0)),
        out_shape=jax.ShapeDtypeStruct(x.shape, F32),
        compiler_params=_cparams(("arbitrary", "arbitrary")),
        name="mixout_a",
    )(x, main, g, w_q, km, vm, gq, w_out)


def _ffn_kernel(x_ref, g_ref, wu_ref, wg_ref, cu_ref, cg_ref, bu_ref, bg_ref, wd_ref, o_ref,
                pad_ref, carry_ref):
    nb = x_ref.shape[0]

    @pl.when(pl.program_id(0) == 0)
    def _():
        carry_ref[...] = jnp.zeros_like(carry_ref)

    hb = _rms(_to_time_major(x_ref, pad_ref), g_ref[...]).astype(BF16)
    n = hb.shape[0]
    keep = carry_ref.shape[1]

    def branch(w_ref, c_ref, b_ref, slot, cols):
        pre = _dot(hb, w_ref[:, cols])
        out = _causal_conv_tm(pre, carry_ref[slot, :, cols], c_ref, b_ref, nb, cols)
        carry_ref[slot, :, cols] = pre[n - keep:, :]
        return out

    acts = []
    for j in range(wu_ref.shape[1] // FFN_SUB):
        cols = slice(j * FFN_SUB, (j + 1) * FFN_SUB)
        u = branch(wu_ref, cu_ref, bu_ref, 0, cols)
        gt = branch(wg_ref, cg_ref, bg_ref, 1, cols)
        acts.append((_gelu(gt) * u).astype(BF16))
    act = jnp.concatenate(acts, axis=1)
    y = jnp.concatenate([_dot(act, wd_ref[:, j * FFN_SUB:(j + 1) * FFN_SUB])
                         for j in range(wd_ref.shape[1] // FFN_SUB)], axis=1)
    _from_time_major(y, o_ref, pad_ref, res_ref=x_ref)


def _ffn_call(x, g, w_up, w_dw, b_dw, w_down):
    bsz, s, d = x.shape
    d_ff = w_down.shape[0]
    tt = TT_FFN
    width = w_dw.shape[0]
    resident = dict(pipeline_mode=pl.Buffered(1))
    return pl.pallas_call(
        _ffn_kernel,
        grid=(s // tt,),
        in_specs=[
            pl.BlockSpec((bsz, tt, d), lambda i: (0, i, 0)),
            pl.BlockSpec((1, d), lambda i: (0, 0)),
            pl.BlockSpec((d, d_ff), lambda i: (0, 0), **resident),
            pl.BlockSpec((d, d_ff), lambda i: (0, 1), **resident),
            pl.BlockSpec((width, d_ff), lambda i: (0, 0)),
            pl.BlockSpec((width, d_ff), lambda i: (0, 1)),
            pl.BlockSpec((1, d_ff), lambda i: (0, 0)),
            pl.BlockSpec((1, d_ff), lambda i: (0, 1)),
            pl.BlockSpec((d_ff, d), lambda i: (0, 0), **resident),
        ],
        out_specs=pl.BlockSpec((bsz, tt, d), lambda i: (0, i, 0)),
        out_shape=jax.ShapeDtypeStruct(x.shape, F32),
        scratch_shapes=[
            pltpu.VMEM((d // LANES, bsz * (tt + SUBLANES), LANES), F32),
            pltpu.VMEM((2, (width - 1) * bsz, d_ff), F32),
        ],
        compiler_params=_cparams(("arbitrary",)),
        name="conv_ffn",
    )(x, g, w_up, w_up, w_dw, w_dw, b_dw, b_dw, w_down)


def _kv_kernel(x_ref, g_ref, wk_ref, wvt_ref, gk_ref, k_ref, vt_ref, *, hd):
    h = _rms(x_ref[...], g_ref[...]).astype(BF16)
    k = _dot(h, wk_ref[...])
    k_ref[...] = _head_rms_rows(k, gk_ref[...], hd).astype(BF16)
    vt = _dot_nt(wvt_ref[...], h)
    ext = hd + BF16_ROWS
    n_blk, _, tq = vt_ref.shape
    ones = jnp.ones((BF16_ROWS, tq), BF16)
    for j in range(n_blk):
        for hh in range(vt.shape[0] // hd):
            vt_ref[j, hh * ext:hh * ext + hd, :] = vt[hh * hd:(hh + 1) * hd, j * tq:(j + 1) * tq].astype(BF16)
            vt_ref[j, hh * ext + hd:(hh + 1) * ext, :] = ones


def _kv_call(x, g, w_k, w_vt, gk_row, hd):
    bsz, s, d = x.shape
    w = w_k.shape[1]
    wx = (w // hd) * (hd + BF16_ROWS)
    ts = TS_KV
    return pl.pallas_call(
        functools.partial(_kv_kernel, hd=hd),
        grid=(bsz, s // ts),
        in_specs=[
            pl.BlockSpec((None, ts, d), lambda b, i: (b, i, 0)),
            pl.BlockSpec((1, d), lambda b, i: (0, 0)),
            pl.BlockSpec((d, w), lambda b, i: (0, 0)),
            pl.BlockSpec((w, d), lambda b, i: (0, 0)),
            pl.BlockSpec((1, w), lambda b, i: (0, 0)),
        ],
        out_specs=[
            pl.BlockSpec((None, ts, w), lambda b, i: (b, i, 0)),
            pl.BlockSpec((None, ts // TQ, wx, TQ), lambda b, i: (b, i, 0, 0)),
        ],
        out_shape=[jax.ShapeDtypeStruct((bsz, s, w), BF16),
                   jax.ShapeDtypeStruct((bsz, s // TQ, wx, TQ), BF16)],
        compiler_params=_cparams(("arbitrary", "arbitrary")),
        name="kv_proj",
    )(x, g, w_k, w_vt, gk_row)


def _bias_kernel(f_ref, o_ref):
    x = jnp.broadcast_to(f_ref[0], (CHUNK, KEY_ROWS))
    r = pltpu.roll(x, 0, 1, stride=1, stride_axis=0)
    band = r.T[CHUNK - 1:CHUNK - 1 + BAND, :]
    negs = jnp.full((CHUNK, CHUNK), NEG, F32)
    lo = jnp.concatenate([band, negs], axis=0)
    hi = jnp.concatenate([negs, band], axis=0)
    o_ref[0, 0:NEG_ROWS, :] = jnp.full((NEG_ROWS, 2 * CHUNK), NEG, F32)
    o_ref[0, NEG_ROWS:, :] = jnp.concatenate([lo, hi], axis=1)


def _bias_call(rel_bias):
    n_heads, n_rel = rel_bias.shape
    assert n_rel == MAX_REL + CHUNK and 2 * CHUNK == LANES
    f = jnp.concatenate([jnp.broadcast_to(rel_bias[:, n_rel - 1:], (n_heads, KEY_ROWS - 1 - n_rel)),
                         rel_bias[:, ::-1], rel_bias[:, :1]], axis=1) * LOG2E
    rows = NEG_ROWS + KEY_ROWS
    return pl.pallas_call(
        _bias_kernel,
        grid=(n_heads,),
        in_specs=[pl.BlockSpec((1, 1, KEY_ROWS), lambda h: (h, 0, 0))],
        out_specs=pl.BlockSpec((1, rows, LANES), lambda h: (h, 0, 0)),
        out_shape=jax.ShapeDtypeStruct((n_heads, rows, LANES), F32),
        compiler_params=_cparams(("arbitrary",)),
        name="band_bias",
    )(f.astype(F32)[:, None, :])


def _mixer_b_kernel(x_ref, g_ref, win_ref, gqa_ref, k0_ref, k1_ref, k2_ref, v0_ref, v1_ref, v2_ref,
                    pb_ref, km_ref, vm_ref, gq_ref, wout_ref, o_ref, qa_ref, s_ref, p_ref, att_ref,
                    *, att_w, att_hd, n_heads, hd):
    i = pl.program_id(1)
    x = x_ref[...]
    tq = x.shape[0]
    half = tq // 2
    assert half == LANES
    h = _rms(x, g_ref[...]).astype(BF16)
    proj = _dot(h, win_ref[...])
    qa_ref[...] = pltpu.bitcast(_head_rms_rows(proj[:, :att_w], gqa_ref[...], att_hd).T.astype(BF16), jnp.uint32)
    qm = proj[:, att_w:]

    k_refs = (k0_ref, k1_ref, k2_ref)
    v_refs = (v0_ref, v1_ref, v2_ref)
    ok0 = i >= 2
    ok1 = i >= 1
    ext = att_hd + BF16_ROWS

    zblk = jnp.zeros((half, half), BF16)
    for slot in range(2):
        p_ref[slot, 0, 0:half, half:] = zblk
        p_ref[slot, 2, half:, 0:half] = zblk

    def bias_rows(hh, ok, start, size):
        first = pl.multiple_of(jnp.where(ok, NEG_ROWS + start, 0), LANES)
        return pb_ref[hh, pl.ds(first, size), :]

    def scores(hh):
        slot = hh % 2
        col = (hh // 2) * LANES
        rows = att_hd // 2
        q_t = pltpu.bitcast(qa_ref[hh * rows:(hh + 1) * rows, :], BF16)
        qz = jnp.concatenate([q_t, jnp.zeros_like(q_t)] if hh % 2 == 0 else [jnp.zeros_like(q_t), q_t], axis=0)
        st = [_dot(kr[:, col:col + LANES], qz) for kr in k_refs]
        s_ref[slot, 0, 0:tq, :] = st[0][:, :half] + bias_rows(hh, ok0, 0, tq)
        s_ref[slot, 0, tq:2 * tq, :] = st[1][:, :half] + bias_rows(hh, ok1, tq, tq)
        s_ref[slot, 0, 2 * tq:, :] = st[2][:half, :half] + pb_ref[hh, NEG_ROWS + 2 * tq:, :]
        s_ref[slot, 1, 0:half, :] = st[0][half:, half:] + bias_rows(hh, ok0, 0, half)
        s_ref[slot, 1, half:half + tq, :] = st[1][:, half:] + bias_rows(hh, ok1, half, tq)
        s_ref[slot, 1, half + tq:, :] = st[2][:, half:] + pb_ref[hh, NEG_ROWS + half + tq:, :]

    def probs(hh):
        slot = hh % 2
        z = s_ref[slot, 0]
        e = jnp.exp2(z - jnp.max(z, axis=0, keepdims=True)).astype(BF16)
        p_ref[slot, 0, :, 0:half] = e[0:tq]
        p_ref[slot, 1, :, 0:half] = e[tq:2 * tq]
        p_ref[slot, 2, 0:half, 0:half] = e[2 * tq:]
        z = s_ref[slot, 1]
        e = jnp.exp2(z - jnp.max(z, axis=0, keepdims=True)).astype(BF16)
        p_ref[slot, 0, half:, half:] = e[0:half]
        p_ref[slot, 1, :, half:] = e[half:half + tq]
        p_ref[slot, 2, :, half:] = e[half + tq:]

    def values(hh):
        slot = hh % 2
        vt_h = jnp.concatenate([vr[hh * ext:(hh + 1) * ext, :] for vr in v_refs], axis=1)
        ot = _dot(vt_h, p_ref[slot].reshape(len(v_refs) * tq, tq))
        att_ref[hh * att_hd:(hh + 1) * att_hd, :] = ot[:att_hd] / ot[att_hd:att_hd + 1]

    n_att = att_w // att_hd
    for step in range(n_att + 2):
        if 0 <= step - 2:
            values(step - 2)
        if 0 <= step - 1 < n_att:
            probs(step - 1)
        if step < n_att:
            scores(step)

    att = att_ref[...].T.astype(BF16)
    mo = _mem_attention(qm, km_ref, vm_ref, gq_ref[...], n_heads, hd).astype(BF16)
    y = _dot(att, wout_ref[:att_w, :]) + _dot(mo, wout_ref[att_w:, :])
    o_ref[...] = x + y


def _mixer_b_call(x, g, w_in, gqa_row, k, vt, pb, km, vm, gq, w_out, att_hd, n_heads, hd):
    bsz, s, d = x.shape
    att_w = k.shape[2]
    vt_w = vt.shape[2]
    n_mem, mw = km.shape[1:]
    tq = TQ
    const = lambda *shape: pl.BlockSpec(shape, lambda b, i: (0,) * len(shape))
    kspec = lambda r: pl.BlockSpec((None, tq, att_w), lambda b, i: (b, jnp.maximum(i - 2 + r, 0), 0))
    vspec = lambda r: pl.BlockSpec((None, None, vt_w, tq), lambda b, i: (b, jnp.maximum(i - 2 + r, 0), 0, 0))
    return pl.pallas_call(
        functools.partial(_mixer_b_kernel, att_w=att_w, att_hd=att_hd, n_heads=n_heads, hd=hd),
        grid=(bsz, s // tq),
        in_specs=[
            pl.BlockSpec((None, tq, d), lambda b, i: (b, i, 0)),
            const(1, d),
            const(*w_in.shape),
            const(1, att_w),
            kspec(0), kspec(1), kspec(2),
            vspec(0), vspec(1), vspec(2),
            const(*pb.shape),
            pl.BlockSpec((None, n_mem, mw), lambda b, i: (b, 0, 0)),
            pl.BlockSpec((None, n_mem, mw), lambda b, i: (b, 0, 0)),
            const(1, hd),
            const(*w_out.shape),
        ],
        out_specs=pl.BlockSpec((None, tq, d), lambda b, i: (b, i, 0)),
        out_shape=jax.ShapeDtypeStruct(x.shape, F32),
        scratch_shapes=[
            pltpu.VMEM((att_w // 2, tq), jnp.uint32),
            pltpu.VMEM((2, 2, KEY_ROWS, LANES), F32),
            pltpu.VMEM((2, 3, tq, tq), BF16),
            pltpu.VMEM((att_w, tq), F32),
        ],
        compiler_params=_cparams(("arbitrary", "arbitrary")),
        name="mixer_b",
    )(x, g, w_in, gqa_row, k, k, k, vt, vt, vt, pb, km, vm, gq, w_out)


def _block_diag_windows(w_r, w_i):
    n, bw, _ = w_r.shape
    width = n * bw
    assert width % MXU_COLS == 0

    def dense(w):
        eye = jnp.eye(n, dtype=w.dtype)
        return jnp.einsum('nkj,nm->nkmj', w, eye).reshape(width, width)

    dr, di = dense(w_r), dense(w_i)
    starts, tiles = [], []
    for j in range(width // MXU_COLS):
        lo = (j * MXU_COLS // bw) * bw
        hi = -((-(j + 1) * MXU_COLS) // bw) * bw
        ws = min((lo // LANES) * LANES, width - BD_WINDOW)
        assert ws <= lo and hi <= ws + BD_WINDOW
        cols = slice(j * MXU_COLS, (j + 1) * MXU_COLS)
        tiles.append(jnp.concatenate([dr[ws:ws + BD_WINDOW, cols], di[ws:ws + BD_WINDOW, cols]], axis=1))
        starts.append(ws)
    return jnp.stack(tiles).astype(BF16), tuple(starts)


def kernel(x, mem, g_mix, g_mem, w_mem_kv, g_q_mem, g_k_mem, w_in_a, w_conv_a, b_conv_a, w_r_a, b_r_a,
           w_i_a, b_i_a, lam_a, w_out_a, g_kv, w_kv, g_k_att, w_in_b, g_q_att, rel_bias_b, w_out_b,
           g_ffn, w_up, w_dw_ffn, b_dw_ffn, w_down):
    bsz, s, d = x.shape
    depth = g_mix.shape[0]
    n_a = w_in_a.shape[0]
    mem_hd = g_q_mem.shape[1]
    mem_w = w_mem_kv.shape[2] // 2
    mem_heads = mem_w // mem_hd
    att_hd = g_k_att.shape[0]
    att_w = w_kv.shape[1] // 2
    att_heads = att_w // att_hd
    d_rnn = w_conv_a.shape[2]
    assert att_hd * 2 == LANES and att_hd == CHUNK and bsz == SUBLANES
    assert s % TS_KV == 0 and s % TS_MIX == 0 and s % TQ == 0 and s % TT_FFN == 0 and s % TT_RGLRU == 0

    row = lambda v: v.reshape(1, -1).astype(F32)
    km_all, vm_all = _memkv_call(mem, g_mem, w_mem_kv, g_k_mem, mem_heads, mem_hd)

    k = vt = None
    for l in range(depth):
        if l < n_a:
            wbd, bd_starts = _block_diag_windows(w_r_a[l], w_i_a[l])
            w_in = w_in_a[l].astype(BF16)
            main = _rglru_call(x, row(g_mix[l]), w_in[:, :2 * d_rnn], w_conv_a[l], row(b_conv_a[l]), wbd,
                               bd_starts, row(b_r_a[l]), row(b_i_a[l]), row(lam_a[l]))
            x = _mixout_call(x, main, row(g_mix[l]), w_in[:, 2 * d_rnn:], km_all[l], vm_all[l],
                             row(g_q_mem[l]), w_out_a[l].astype(BF16), mem_heads, mem_hd)
        else:
            j = l - n_a
            if j == 0:
                k, vt = _kv_call(x, row(g_kv), w_kv[:, :att_w].astype(BF16), w_kv[:, att_w:].T.astype(BF16),
                                 row(jnp.tile(g_k_att, att_heads)), att_hd)
            gqa = row(jnp.tile(g_q_att[j], att_heads) * (att_hd ** -0.5 * LOG2E))
            x = _mixer_b_call(x, row(g_mix[l]), w_in_b[j].astype(BF16), gqa, k, vt, _bias_call(rel_bias_b[j]),
                              km_all[l], vm_all[l], row(g_q_mem[l]), w_out_b[j].astype(BF16),
                              att_hd, mem_heads, mem_hd)
        x = _ffn_call(x, row(g_ffn[l]), w_up[l].astype(BF16), w_dw_ffn[l], row(b_dw_ffn[l]),
                      w_down[l].astype(BF16))
    return x
```

```python
import functools
import math

import jax
import jax.numpy as jnp
from jax import lax
from jax.experimental import pallas as pl
from jax.experimental.pallas import tpu as pltpu

EPS = 1e-6
CHUNK = 64
N_PREV_CHUNKS = 8
BAND = (N_PREV_CHUNKS + 1) * CHUNK
MAX_REL = 256
RG_C = 8.0
NEG = -1e30
LOG2E = math.log2(math.e)

LANES = 128
SUBLANES = 8
BF16_ROWS = 16
MXU_COLS = 256
BD_WINDOW = 512
VMEM_LIMIT = 56 * 1024 * 1024

TT_RGLRU = 32
TT_FFN = 64
FFN_SUB = 256
TS_MIX = 512
TS_KV = 512
TQ = 4 * CHUNK
KEY_ROWS = BAND + CHUNK
NEG_ROWS = TQ

F32 = jnp.float32
BF16 = jnp.bfloat16


def _cparams(sem):
    return pltpu.CompilerParams(dimension_semantics=sem, vmem_limit_bytes=VMEM_LIMIT)


def _dot(a, b):
    return jnp.dot(a, b, preferred_element_type=F32)


def _dot_nt(a, b):
    return lax.dot_general(a, b, (((1,), (1,)), ((), ())), preferred_element_type=F32)


def _rms(x, g):
    ms = jnp.mean(x * x, axis=-1, keepdims=True)
    return x * lax.rsqrt(ms + EPS) * g


def _gelu(x):
    c = -2.0 * 0.7978845608028654 * LOG2E
    return x / (1.0 + jnp.exp2(x * (c + (c * 0.044715) * (x * x))))


def _gelu_tanh(x):
    c = 0.7978845608028654
    return x * (0.5 + 0.5 * jnp.tanh(x * (c + (c * 0.044715) * (x * x))))


def _sigmoid(x):
    return 0.5 + 0.5 * jnp.tanh(0.5 * x)


def _to_time_major(x_ref, pad_ref):
    nb, tt, d = x_ref.shape
    pitch = pad_ref.shape[1] // nb
    for b in range(nb):
        for c in range(d // LANES):
            pad_ref[c, b * pitch:b * pitch + tt, :] = x_ref[b, :, c * LANES:(c + 1) * LANES].astype(F32)
    rows = []
    for t in range(tt):
        rows.append(jnp.concatenate(
            [pad_ref[c, pl.ds(t, nb, stride=pitch), :] for c in range(d // LANES)], axis=1))
    return jnp.concatenate(rows, axis=0)


def _from_time_major(y, o_ref, pad_ref, res_ref=None):
    nb, tt, d = o_ref.shape
    pitch = pad_ref.shape[1] // nb
    for t in range(tt):
        for c in range(d // LANES):
            pad_ref[c, pl.ds(t, nb, stride=pitch), :] = y[t * nb:(t + 1) * nb, c * LANES:(c + 1) * LANES]
    for b in range(nb):
        for c in range(d // LANES):
            v = pad_ref[c, b * pitch:b * pitch + tt, :]
            if res_ref is not None:
                v = v + res_ref[b, :, c * LANES:(c + 1) * LANES]
            o_ref[b, :, c * LANES:(c + 1) * LANES] = v.astype(o_ref.dtype)


def _causal_conv_tm(x, prev, w_ref, b_ref, nb, cols=slice(None)):
    n = x.shape[0]
    width = w_ref.shape[0]
    xc = jnp.concatenate([prev, x], axis=0)
    out = x * w_ref[width - 1:width, cols] + b_ref[:, cols]
    for k in range(width - 1):
        out = out + xc[k * nb:k * nb + n] * w_ref[k:k + 1, cols]
    return out


def _head_rms_rows(x, g, hd):
    t, w = x.shape
    lane = lax.broadcasted_iota(jnp.int32, (t, LANES), 1)
    lo = lane < hd
    outs = []
    for p in range(w // LANES):
        xp = x[:, p * LANES:(p + 1) * LANES]
        x2 = xp * xp
        s_lo = jnp.sum(jnp.where(lo, x2, 0.0), axis=-1, keepdims=True)
        s_hi = jnp.sum(jnp.where(lo, 0.0, x2), axis=-1, keepdims=True)
        r_lo = lax.rsqrt(s_lo * (1.0 / hd) + EPS)
        r_hi = lax.rsqrt(s_hi * (1.0 / hd) + EPS)
        outs.append(xp * jnp.where(lo, r_lo, r_hi))
    return jnp.concatenate(outs, axis=-1) * g


def _mem_attention(qm, km_ref, vm_ref, gq, n_heads, hd):
    outs = []
    for hh in range(n_heads):
        sl = slice(hh * hd, (hh + 1) * hd)
        qn = (_rms(qm[:, sl], gq) * (hd ** -0.5)).astype(BF16)
        s = _dot_nt(qn, km_ref[:, sl])
        m = jnp.max(s, axis=-1, keepdims=True)
        e = jnp.exp(s - m)
        l = jnp.sum(e, axis=-1, keepdims=True)
        o = _dot(e.astype(BF16), vm_ref[:, sl])
        outs.append(o / l)
    return jnp.concatenate(outs, axis=-1)


def _memkv_kernel(mem_ref, g_ref, w_ref, gk_ref, km_ref, vm_ref, *, n_heads, hd):
    mn = _rms(mem_ref[...], g_ref[...]).astype(BF16)
    kv = _dot(mn, w_ref[...])
    w = n_heads * hd
    ks = []
    for hh in range(n_heads):
        ks.append(_rms(kv[:, hh * hd:(hh + 1) * hd], gk_ref[...]))
    km_ref[...] = jnp.concatenate(ks, axis=-1).astype(BF16)
    vm_ref[...] = kv[:, w:].astype(BF16)


def _memkv_call(mem, g_mem, w_mem_kv, g_k_mem, n_heads, hd):
    depth = g_mem.shape[0]
    bsz, n_mem, d = mem.shape
    w = n_heads * hd
    out = jax.ShapeDtypeStruct((depth, bsz, n_mem, w), BF16)
    return pl.pallas_call(
        functools.partial(_memkv_kernel, n_heads=n_heads, hd=hd),
        grid=(depth, bsz),
        in_specs=[
            pl.BlockSpec((None, n_mem, d), lambda l, b: (b, 0, 0)),
            pl.BlockSpec((None, 1, d), lambda l, b: (l, 0, 0)),
            pl.BlockSpec((None, d, 2 * w), lambda l, b: (l, 0, 0)),
            pl.BlockSpec((None, 1, hd), lambda l, b: (l, 0, 0)),
        ],
        out_specs=[
            pl.BlockSpec((None, None, n_mem, w), lambda l, b: (l, b, 0, 0)),
            pl.BlockSpec((None, None, n_mem, w), lambda l, b: (l, b, 0, 0)),
        ],
        out_shape=[out, out],
        compiler_params=_cparams(("arbitrary", "arbitrary")),
        name="mem_kv",
    )(mem, g_mem.reshape(depth, 1, d), w_mem_kv.astype(BF16), g_k_mem.reshape(depth, 1, hd))


def _rglru_kernel(x_ref, g_ref, win_ref, wc_ref, bc_ref, wbd_ref, br_ref, bi_ref, lam_ref, o_ref,
                  pad_ref, xprev_ref, hprev_ref, *, bd_starts):
    @pl.when(pl.program_id(0) == 0)
    def _():
        xprev_ref[...] = jnp.zeros_like(xprev_ref)
        hprev_ref[...] = jnp.zeros_like(hprev_ref)

    nb, tt, _ = x_ref.shape
    d_rnn = wc_ref.shape[1]
    x = _to_time_major(x_ref, pad_ref)
    n = x.shape[0]
    h = _rms(x, g_ref[...]).astype(BF16)
    proj = _dot(h, win_ref[...])
    xr = proj[:, :d_rnn]
    yg = proj[:, d_rnn:]

    conv = _causal_conv_tm(xr, xprev_ref[...], wc_ref, bc_ref, nb)
    xprev_ref[...] = xr[n - xprev_ref.shape[0]:, :]

    cb = conv.astype(BF16)
    win = wbd_ref.shape[1]
    ris = [_dot(cb[:, ws:ws + win], wbd_ref[j]) for j, ws in enumerate(bd_starts)]
    r = _sigmoid(jnp.concatenate([ri[:, :MXU_COLS] for ri in ris], axis=-1) + br_ref[...])
    ig = _sigmoid(jnp.concatenate([ri[:, MXU_COLS:] for ri in ris], axis=-1) + bi_ref[...])
    lam = lam_ref[...]
    neg_c = -RG_C * (jnp.minimum(lam, 0.0) - jnp.log1p(jnp.exp(-jnp.abs(lam))))
    neg_log_a = r * neg_c
    a = jnp.exp2(r * (neg_c * -LOG2E))
    z = jnp.tanh(neg_log_a) * (a * a + 1.0)
    u = jnp.where(z > 0.0, z * lax.rsqrt(z), 0.0) * ig * conv

    hcur = hprev_ref[...]
    hs = []
    for t in range(tt):
        hcur = a[t * nb:(t + 1) * nb] * hcur + u[t * nb:(t + 1) * nb]
        hs.append(hcur)
    hprev_ref[...] = hcur
    main = jnp.concatenate(hs, axis=0) * _gelu_tanh(yg)
    _from_time_major(main, o_ref, pad_ref)


def _rglru_call(x, g, w_in_xy, w_conv, b_conv, wbd, bd_starts, b_r, b_i, lam):
    bsz, s, d = x.shape
    d_rnn = w_conv.shape[1]
    tt = TT_RGLRU
    const = lambda *shape: pl.BlockSpec(shape, lambda i: (0,) * len(shape))
    return pl.pallas_call(
        functools.partial(_rglru_kernel, bd_starts=bd_starts),
        grid=(s // tt,),
        in_specs=[
            pl.BlockSpec((bsz, tt, d), lambda i: (0, i, 0)),
            const(1, d),
            const(*w_in_xy.shape),
            const(*w_conv.shape),
            const(1, d_rnn),
            const(*wbd.shape),
            const(1, d_rnn),
            const(1, d_rnn),
            const(1, d_rnn),
        ],
        out_specs=pl.BlockSpec((bsz, tt, d_rnn), lambda i: (0, i, 0)),
        out_shape=jax.ShapeDtypeStruct((bsz, s, d_rnn), BF16),
        scratch_shapes=[
            pltpu.VMEM((d_rnn // LANES, bsz * (tt + SUBLANES), LANES), F32),
            pltpu.VMEM(((w_conv.shape[0] - 1) * bsz, d_rnn), F32),
            pltpu.VMEM((bsz, d_rnn), F32),
        ],
        compiler_params=_cparams(("arbitrary",)),
        name="rglru_a",
    )(x, g, w_in_xy, w_conv, b_conv, wbd, b_r, b_i, lam)


def _mixout_kernel(x_ref, main_ref, g_ref, wq_ref, km_ref, vm_ref, gq_ref, wout_ref, o_ref, *, n_heads, hd):
    x = x_ref[...]
    d_main = main_ref.shape[1]
    h = _rms(x, g_ref[...]).astype(BF16)
    qm = _dot(h, wq_ref[...])
    mo = _mem_attention(qm, km_ref, vm_ref, gq_ref[...], n_heads, hd).astype(BF16)
    y = _dot(main_ref[...], wout_ref[:d_main, :]) + _dot(mo, wout_ref[d_main:, :])
    o_ref[...] = x + y


def _mixout_call(x, main, g, w_q, km, vm, gq, w_out, n_heads, hd):
    bsz, s, d = x.shape
    d_main = main.shape[2]
    n_mem, mw = km.shape[1:]
    ts = TS_MIX
    const = lambda *shape: pl.BlockSpec(shape, lambda b, i: (0,) * len(shape))
    return pl.pallas_call(
        functools.partial(_mixout_kernel, n_heads=n_heads, hd=hd),
        grid=(bsz, s // ts),
        in_specs=[
            pl.BlockSpec((None, ts, d), lambda b, i: (b, i, 0)),
            pl.BlockSpec((None, ts, d_main), lambda b, i: (b, i, 0)),
            const(1, d),
            const(*w_q.shape),
            pl.BlockSpec((None, n_mem, mw), lambda b, i: (b, 0, 0)),
            pl.BlockSpec((None, n_mem, mw), lambda b, i: (b, 0, 0)),
            const(1, hd),
            const(*w_out.shape),
        ],
        out_specs=pl.BlockSpec((None, ts, d), lambda b, i: (b, i, 0)),
        out_shape=jax.ShapeDtypeStruct(x.shape, F32),
        compiler_params=_cparams(("arbitrary", "arbitrary")),
        name="mixout_a",
    )(x, main, g, w_q, km, vm, gq, w_out)


def _ffn_kernel(x_ref, g_ref, wu_ref, wg_ref, cu_ref, cg_ref, bu_ref, bg_ref, wd_ref, o_ref,
                pad_ref, carry_ref):
    nb = x_ref.shape[0]

    @pl.when(pl.program_id(0) == 0)
    def _():
        carry_ref[...] = jnp.zeros_like(carry_ref)

    hb = _rms(_to_time_major(x_ref, pad_ref), g_ref[...]).astype(BF16)
    n = hb.shape[0]
    keep = carry_ref.shape[1]

    def branch(w_ref, c_ref, b_ref, slot, cols):
        pre = _dot(hb, w_ref[:, cols])
        out = _causal_conv_tm(pre, carry_ref[slot, :, cols], c_ref, b_ref, nb, cols)
        carry_ref[slot, :, cols] = pre[n - keep:, :]
        return out

    acts = []
    for j in range(wu_ref.shape[1] // FFN_SUB):
        cols = slice(j * FFN_SUB, (j + 1) * FFN_SUB)
        u = branch(wu_ref, cu_ref, bu_ref, 0, cols)
        gt = branch(wg_ref, cg_ref, bg_ref, 1, cols)
        acts.append((_gelu(gt) * u).astype(BF16))
    act = jnp.concatenate(acts, axis=1)
    y = jnp.concatenate([_dot(act, wd_ref[:, j * FFN_SUB:(j + 1) * FFN_SUB])
                         for j in range(wd_ref.shape[1] // FFN_SUB)], axis=1)
    _from_time_major(y, o_ref, pad_ref, res_ref=x_ref)


def _ffn_call(x, layer, g, w_up, w_dw, b_dw, w_down):
    bsz, s, d = x.shape
    d_ff = w_down.shape[1]
    tt = TT_FFN
    width = w_dw.shape[1]
    resident = dict(pipeline_mode=pl.Buffered(1))
    return pl.pallas_call(
        _ffn_kernel,
        grid=(s // tt,),
        in_specs=[
            pl.BlockSpec((bsz, tt, d), lambda i: (0, i, 0)),
            pl.BlockSpec((None, 1, d), lambda i: (layer, 0, 0)),
            pl.BlockSpec((None, d, d_ff), lambda i: (layer, 0, 0), **resident),
            pl.BlockSpec((None, d, d_ff), lambda i: (layer, 0, 1), **resident),
            pl.BlockSpec((None, width, d_ff), lambda i: (layer, 0, 0)),
            pl.BlockSpec((None, width, d_ff), lambda i: (layer, 0, 1)),
            pl.BlockSpec((None, 1, d_ff), lambda i: (layer, 0, 0)),
            pl.BlockSpec((None, 1, d_ff), lambda i: (layer, 0, 1)),
            pl.BlockSpec((None, d_ff, d), lambda i: (layer, 0, 0), **resident),
        ],
        out_specs=pl.BlockSpec((bsz, tt, d), lambda i: (0, i, 0)),
        out_shape=jax.ShapeDtypeStruct(x.shape, F32),
        scratch_shapes=[
            pltpu.VMEM((d // LANES, bsz * (tt + SUBLANES), LANES), F32),
            pltpu.VMEM((2, (width - 1) * bsz, d_ff), F32),
        ],
        compiler_params=_cparams(("arbitrary",)),
        name="conv_ffn",
    )(x, g, w_up, w_up, w_dw, w_dw, b_dw, b_dw, w_down)


def _kv_kernel(x_ref, g_ref, wk_ref, wvt_ref, gk_ref, k_ref, vt_ref, *, hd):
    h = _rms(x_ref[...], g_ref[...]).astype(BF16)
    k = _dot(h, wk_ref[...])
    k_ref[...] = _head_rms_rows(k, gk_ref[...], hd).astype(BF16)
    vt = _dot_nt(wvt_ref[...], h)
    ext = hd + BF16_ROWS
    n_blk, _, tq = vt_ref.shape
    ones = jnp.ones((BF16_ROWS, tq), BF16)
    for j in range(n_blk):
        for hh in range(vt.shape[0] // hd):
            vt_ref[j, hh * ext:hh * ext + hd, :] = vt[hh * hd:(hh + 1) * hd, j * tq:(j + 1) * tq].astype(BF16)
            vt_ref[j, hh * ext + hd:(hh + 1) * ext, :] = ones


def _kv_call(x, g, w_k, w_vt, gk_row, hd):
    bsz, s, d = x.shape
    w = w_k.shape[1]
    wx = (w // hd) * (hd + BF16_ROWS)
    ts = TS_KV
    return pl.pallas_call(
        functools.partial(_kv_kernel, hd=hd),
        grid=(bsz, s // ts),
        in_specs=[
            pl.BlockSpec((None, ts, d), lambda b, i: (b, i, 0)),
            pl.BlockSpec((1, d), lambda b, i: (0, 0)),
            pl.BlockSpec((d, w), lambda b, i: (0, 0)),
            pl.BlockSpec((w, d), lambda b, i: (0, 0)),
            pl.BlockSpec((1, w), lambda b, i: (0, 0)),
        ],
        out_specs=[
            pl.BlockSpec((None, ts, w), lambda b, i: (b, i, 0)),
            pl.BlockSpec((None, ts // TQ, wx, TQ), lambda b, i: (b, i, 0, 0)),
        ],
        out_shape=[jax.ShapeDtypeStruct((bsz, s, w), BF16),
                   jax.ShapeDtypeStruct((bsz, s // TQ, wx, TQ), BF16)],
        compiler_params=_cparams(("arbitrary", "arbitrary")),
        name="kv_proj",
    )(x, g, w_k, w_vt, gk_row)


def _bias_kernel(f_ref, o_ref):
    x = jnp.broadcast_to(f_ref[0], (CHUNK, KEY_ROWS))
    r = pltpu.roll(x, 0, 1, stride=1, stride_axis=0)
    band = r.T[CHUNK - 1:CHUNK - 1 + BAND, :]
    negs = jnp.full((CHUNK, CHUNK), NEG, F32)
    lo = jnp.concatenate([band, negs], axis=0)
    hi = jnp.concatenate([negs, band], axis=0)
    o_ref[0, 0:NEG_ROWS, :] = jnp.full((NEG_ROWS, 2 * CHUNK), NEG, F32)
    o_ref[0, NEG_ROWS:, :] = jnp.concatenate([lo, hi], axis=1)


def _bias_call(rel_bias):
    n_heads, n_rel = rel_bias.shape
    assert n_rel == MAX_REL + CHUNK and 2 * CHUNK == LANES
    f = jnp.concatenate([jnp.broadcast_to(rel_bias[:, n_rel - 1:], (n_heads, KEY_ROWS - 1 - n_rel)),
                         rel_bias[:, ::-1], rel_bias[:, :1]], axis=1) * LOG2E
    rows = NEG_ROWS + KEY_ROWS
    return pl.pallas_call(
        _bias_kernel,
        grid=(n_heads,),
        in_specs=[pl.BlockSpec((1, 1, KEY_ROWS), lambda h: (h, 0, 0))],
        out_specs=pl.BlockSpec((1, rows, LANES), lambda h: (h, 0, 0)),
        out_shape=jax.ShapeDtypeStruct((n_heads, rows, LANES), F32),
        compiler_params=_cparams(("arbitrary",)),
        name="band_bias",
    )(f.astype(F32)[:, None, :])


def _mixer_b_kernel(x_ref, g_ref, win_ref, gqa_ref, k0_ref, k1_ref, k2_ref, v0_ref, v1_ref, v2_ref,
                    pb_ref, km_ref, vm_ref, gq_ref, wout_ref, o_ref, qa_ref, s_ref, p_ref, att_ref,
                    *, att_w, att_hd, n_heads, hd):
    i = pl.program_id(1)
    x = x_ref[...]
    tq = x.shape[0]
    half = tq // 2
    assert half == LANES
    h = _rms(x, g_ref[...]).astype(BF16)
    proj = _dot(h, win_ref[...])
    q_t = proj[:, :att_w].T
    qn = []
    for hh in range(att_w // att_hd):
        q_h = q_t[hh * att_hd:(hh + 1) * att_hd]
        ms = jnp.mean(q_h * q_h, axis=0, keepdims=True)
        qn.append(q_h * lax.rsqrt(ms + EPS))
    qa_ref[...] = pltpu.bitcast((jnp.concatenate(qn, axis=0) * gqa_ref[...]).astype(BF16), jnp.uint32)
    qm = proj[:, att_w:]

    k_refs = (k0_ref, k1_ref, k2_ref)
    v_refs = (v0_ref, v1_ref, v2_ref)
    ok0 = i >= 2
    ok1 = i >= 1
    ext = att_hd + BF16_ROWS

    zblk = jnp.zeros((half, half), BF16)
    for slot in range(2):
        p_ref[slot, 0, 0:half, half:] = zblk
        p_ref[slot, 2, half:, 0:half] = zblk

    def bias_rows(hh, ok, start, size):
        first = pl.multiple_of(jnp.where(ok, NEG_ROWS + start, 0), LANES)
        return pb_ref[hh, pl.ds(first, size), :]

    def scores(hh):
        slot = hh % 2
        col = (hh // 2) * LANES
        rows = att_hd // 2
        q_t = pltpu.bitcast(qa_ref[hh * rows:(hh + 1) * rows, :], BF16)
        qz = jnp.concatenate([q_t, jnp.zeros_like(q_t)] if hh % 2 == 0 else [jnp.zeros_like(q_t), q_t], axis=0)
        st = [_dot(kr[:, col:col + LANES], qz) for kr in k_refs]
        s_ref[slot, 0, 0:tq, :] = st[0][:, :half] + bias_rows(hh, ok0, 0, tq)
        s_ref[slot, 0, tq:2 * tq, :] = st[1][:, :half] + bias_rows(hh, ok1, tq, tq)
        s_ref[slot, 0, 2 * tq:, :] = st[2][:half, :half] + pb_ref[hh, NEG_ROWS + 2 * tq:, :]
        s_ref[slot, 1, 0:half, :] = st[0][half:, half:] + bias_rows(hh, ok0, 0, half)
        s_ref[slot, 1, half:half + tq, :] = st[1][:, half:] + bias_rows(hh, ok1, half, tq)
        s_ref[slot, 1, half + tq:, :] = st[2][:, half:] + pb_ref[hh, NEG_ROWS + half + tq:, :]

    def probs(hh):
        slot = hh % 2
        z = s_ref[slot, 0]
        e = jnp.exp2(z - jnp.max(z, axis=0, keepdims=True)).astype(BF16)
        p_ref[slot, 0, :, 0:half] = e[0:tq]
        p_ref[slot, 1, :, 0:half] = e[tq:2 * tq]
        p_ref[slot, 2, 0:half, 0:half] = e[2 * tq:]
        z = s_ref[slot, 1]
        e = jnp.exp2(z - jnp.max(z, axis=0, keepdims=True)).astype(BF16)
        p_ref[slot, 0, half:, half:] = e[0:half]
        p_ref[slot, 1, :, half:] = e[half:half + tq]
        p_ref[slot, 2, :, half:] = e[half + tq:]

    def values(hh):
        slot = hh % 2
        vt_h = jnp.concatenate([vr[hh * ext:(hh + 1) * ext, :] for vr in v_refs], axis=1)
        ot = _dot(vt_h, p_ref[slot].reshape(len(v_refs) * tq, tq))
        att_ref[hh * att_hd:(hh + 1) * att_hd, :] = ot[:att_hd] / ot[att_hd:att_hd + 1]

    n_att = att_w // att_hd
    for step in range(n_att + 2):
        if 0 <= step - 2:
            values(step - 2)
        if 0 <= step - 1 < n_att:
            probs(step - 1)
        if step < n_att:
            scores(step)

    att = att_ref[...].T.astype(BF16)
    mo = _mem_attention(qm, km_ref, vm_ref, gq_ref[...], n_heads, hd).astype(BF16)
    y = _dot(att, wout_ref[:att_w, :]) + _dot(mo, wout_ref[att_w:, :])
    o_ref[...] = x + y


def _mixer_b_call(x, g, w_in, gqa_row, k, vt, pb, km, vm, gq, w_out, att_hd, n_heads, hd):
    bsz, s, d = x.shape
    att_w = k.shape[2]
    vt_w = vt.shape[2]
    n_mem, mw = km.shape[1:]
    tq = TQ
    const = lambda *shape: pl.BlockSpec(shape, lambda b, i: (0,) * len(shape))
    kspec = lambda r: pl.BlockSpec((None, tq, att_w), lambda b, i: (b, jnp.maximum(i - 2 + r, 0), 0))
    vspec = lambda r: pl.BlockSpec((None, None, vt_w, tq), lambda b, i: (b, jnp.maximum(i - 2 + r, 0), 0, 0))
    return pl.pallas_call(
        functools.partial(_mixer_b_kernel, att_w=att_w, att_hd=att_hd, n_heads=n_heads, hd=hd),
        grid=(bsz, s // tq),
        in_specs=[
            pl.BlockSpec((None, tq, d), lambda b, i: (b, i, 0)),
            const(1, d),
            const(*w_in.shape),
            const(att_w, tq),
            kspec(0), kspec(1), kspec(2),
            vspec(0), vspec(1), vspec(2),
            const(*pb.shape),
            pl.BlockSpec((None, n_mem, mw), lambda b, i: (b, 0, 0)),
            pl.BlockSpec((None, n_mem, mw), lambda b, i: (b, 0, 0)),
            const(1, hd),
            const(*w_out.shape),
        ],
        out_specs=pl.BlockSpec((None, tq, d), lambda b, i: (b, i, 0)),
        out_shape=jax.ShapeDtypeStruct(x.shape, F32),
        scratch_shapes=[
            pltpu.VMEM((att_w // 2, tq), jnp.uint32),
            pltpu.VMEM((2, 2, KEY_ROWS, LANES), F32),
            pltpu.VMEM((2, 3, tq, tq), BF16),
            pltpu.VMEM((att_w, tq), F32),
        ],
        compiler_params=_cparams(("arbitrary", "arbitrary")),
        name="mixer_b",
    )(x, g, w_in, gqa_row, k, k, k, vt, vt, vt, pb, km, vm, gq, w_out)


def _block_diag_windows(w_r, w_i):
    n, bw, _ = w_r.shape
    width = n * bw
    assert width % MXU_COLS == 0

    def dense(w):
        eye = jnp.eye(n, dtype=w.dtype)
        return jnp.einsum('nkj,nm->nkmj', w, eye).reshape(width, width)

    dr, di = dense(w_r), dense(w_i)
    starts, tiles = [], []
    for j in range(width // MXU_COLS):
        lo = (j * MXU_COLS // bw) * bw
        hi = -((-(j + 1) * MXU_COLS) // bw) * bw
        ws = min((lo // LANES) * LANES, width - BD_WINDOW)
        assert ws <= lo and hi <= ws + BD_WINDOW
        cols = slice(j * MXU_COLS, (j + 1) * MXU_COLS)
        tiles.append(jnp.concatenate([dr[ws:ws + BD_WINDOW, cols], di[ws:ws + BD_WINDOW, cols]], axis=1))
        starts.append(ws)
    return jnp.stack(tiles).astype(BF16), tuple(starts)


def kernel(x, mem, g_mix, g_mem, w_mem_kv, g_q_mem, g_k_mem, w_in_a, w_conv_a, b_conv_a, w_r_a, b_r_a,
           w_i_a, b_i_a, lam_a, w_out_a, g_kv, w_kv, g_k_att, w_in_b, g_q_att, rel_bias_b, w_out_b,
           g_ffn, w_up, w_dw_ffn, b_dw_ffn, w_down):
    bsz, s, d = x.shape
    depth = g_mix.shape[0]
    n_a = w_in_a.shape[0]
    mem_hd = g_q_mem.shape[1]
    mem_w = w_mem_kv.shape[2] // 2
    mem_heads = mem_w // mem_hd
    att_hd = g_k_att.shape[0]
    att_w = w_kv.shape[1] // 2
    att_heads = att_w // att_hd
    d_rnn = w_conv_a.shape[2]
    assert att_hd * 2 == LANES and att_hd == CHUNK and bsz == SUBLANES
    assert s % TS_KV == 0 and s % TS_MIX == 0 and s % TQ == 0 and s % TT_FFN == 0 and s % TT_RGLRU == 0

    row = lambda v: v.reshape(1, -1).astype(F32)
    km_all, vm_all = _memkv_call(mem, g_mem, w_mem_kv, g_k_mem, mem_heads, mem_hd)

    g_ffn3 = g_ffn[:, None, :].astype(F32)
    b_dw3 = b_dw_ffn[:, None, :].astype(F32)
    w_up_bf = w_up.astype(BF16)
    w_down_bf = w_down.astype(BF16)
    k = vt = None
    for l in range(depth):
        if l < n_a:
            wbd, bd_starts = _block_diag_windows(w_r_a[l], w_i_a[l])
            w_in = w_in_a[l].astype(BF16)
            main = _rglru_call(x, row(g_mix[l]), w_in[:, :2 * d_rnn], w_conv_a[l], row(b_conv_a[l]), wbd,
                               bd_starts, row(b_r_a[l]), row(b_i_a[l]), row(lam_a[l]))
            x = _mixout_call(x, main, row(g_mix[l]), w_in[:, 2 * d_rnn:], km_all[l], vm_all[l],
                             row(g_q_mem[l]), w_out_a[l].astype(BF16), mem_heads, mem_hd)
        else:
            j = l - n_a
            if j == 0:
                k, vt = _kv_call(x, row(g_kv), w_kv[:, :att_w].astype(BF16), w_kv[:, att_w:].T.astype(BF16),
                                 row(jnp.tile(g_k_att, att_heads)), att_hd)
            gqa = jnp.broadcast_to((jnp.tile(g_q_att[j], att_heads) * (att_hd ** -0.5 * LOG2E))[:, None],
                                   (att_w, TQ)).astype(F32)
            x = _mixer_b_call(x, row(g_mix[l]), w_in_b[j].astype(BF16), gqa, k, vt, _bias_call(rel_bias_b[j]),
                              km_all[l], vm_all[l], row(g_q_mem[l]), w_out_b[j].astype(BF16),
                              att_hd, mem_heads, mem_hd)
        x = _ffn_call(x, l, g_ffn3, w_up_bf, w_dw_ffn, b_dw3, w_down_bf)
    return x
```

```python
import functools
import math

import jax
import jax.numpy as jnp
from jax import lax
from jax.experimental import pallas as pl
from jax.experimental.pallas import tpu as pltpu

EPS = 1e-6
CHUNK = 64
N_PREV_CHUNKS = 8
BAND = (N_PREV_CHUNKS + 1) * CHUNK
MAX_REL = 256
RG_C = 8.0
NEG = -1e30
LOG2E = math.log2(math.e)

LANES = 128
SUBLANES = 8
BF16_ROWS = 16
MXU_COLS = 256
BD_WINDOW = 512
VMEM_LIMIT = 56 * 1024 * 1024

TT_RGLRU = 32
TT_FFN = 64
FFN_SUB = 256
TS_MIX = 1024
TS_KV = 1024
TQ = 4 * CHUNK
KEY_ROWS = BAND + CHUNK
NEG_ROWS = TQ

F32 = jnp.float32
BF16 = jnp.bfloat16


def _cparams(sem):
    return pltpu.CompilerParams(dimension_semantics=sem, vmem_limit_bytes=VMEM_LIMIT)


def _dot(a, b):
    return jnp.dot(a, b, preferred_element_type=F32)


def _dot_nt(a, b):
    return lax.dot_general(a, b, (((1,), (1,)), ((), ())), preferred_element_type=F32)


def _rms(x, g):
    ms = jnp.mean(x * x, axis=-1, keepdims=True)
    return x * lax.rsqrt(ms + EPS) * g


def _gelu(x):
    c = -2.0 * 0.7978845608028654 * LOG2E
    return x / (1.0 + jnp.exp2(x * (c + (c * 0.044715) * (x * x))))


def _gelu_tanh(x):
    c = 0.7978845608028654
    return x * (0.5 + 0.5 * jnp.tanh(x * (c + (c * 0.044715) * (x * x))))


def _sigmoid(x):
    return 0.5 + 0.5 * jnp.tanh(0.5 * x)


def _to_time_major(x_ref, pad_ref):
    nb, tt, d = x_ref.shape
    pitch = pad_ref.shape[1] // nb
    for b in range(nb):
        for c in range(d // LANES):
            pad_ref[c, b * pitch:b * pitch + tt, :] = x_ref[b, :, c * LANES:(c + 1) * LANES].astype(F32)
    rows = []
    for t in range(tt):
        rows.append(jnp.concatenate(
            [pad_ref[c, pl.ds(t, nb, stride=pitch), :] for c in range(d // LANES)], axis=1))
    return jnp.concatenate(rows, axis=0)


def _from_time_major(y, o_ref, pad_ref, res_ref=None):
    nb, tt, d = o_ref.shape
    pitch = pad_ref.shape[1] // nb
    for t in range(tt):
        for c in range(d // LANES):
            pad_ref[c, pl.ds(t, nb, stride=pitch), :] = y[t * nb:(t + 1) * nb, c * LANES:(c + 1) * LANES]
    for b in range(nb):
        for c in range(d // LANES):
            v = pad_ref[c, b * pitch:b * pitch + tt, :]
            if res_ref is not None:
                v = v + res_ref[b, :, c * LANES:(c + 1) * LANES]
            o_ref[b, :, c * LANES:(c + 1) * LANES] = v.astype(o_ref.dtype)


def _causal_conv_tm(x, prev, w_ref, b_ref, nb, cols=slice(None)):
    n = x.shape[0]
    width = w_ref.shape[0]
    xc = jnp.concatenate([prev, x], axis=0)
    out = x * w_ref[width - 1:width, cols] + b_ref[:, cols]
    for k in range(width - 1):
        out = out + xc[k * nb:k * nb + n] * w_ref[k:k + 1, cols]
    return out


def _head_rms_rows(x, g, hd):
    t, w = x.shape
    lane = lax.broadcasted_iota(jnp.int32, (t, LANES), 1)
    lo = lane < hd
    outs = []
    for p in range(w // LANES):
        xp = x[:, p * LANES:(p + 1) * LANES]
        x2 = xp * xp
        s_lo = jnp.sum(jnp.where(lo, x2, 0.0), axis=-1, keepdims=True)
        s_hi = jnp.sum(jnp.where(lo, 0.0, x2), axis=-1, keepdims=True)
        r_lo = lax.rsqrt(s_lo * (1.0 / hd) + EPS)
        r_hi = lax.rsqrt(s_hi * (1.0 / hd) + EPS)
        outs.append(xp * jnp.where(lo, r_lo, r_hi))
    return jnp.concatenate(outs, axis=-1) * g


def _mem_attention(qm, km_ref, vm_ref, gq, n_heads, hd):
    outs = []
    for hh in range(n_heads):
        sl = slice(hh * hd, (hh + 1) * hd)
        qn = (_rms(qm[:, sl], gq) * (hd ** -0.5)).astype(BF16)
        s = _dot_nt(qn, km_ref[:, sl])
        m = jnp.max(s, axis=-1, keepdims=True)
        e = jnp.exp(s - m)
        l = jnp.sum(e, axis=-1, keepdims=True)
        o = _dot(e.astype(BF16), vm_ref[:, sl])
        outs.append(o / l)
    return jnp.concatenate(outs, axis=-1)


def _memkv_kernel(mem_ref, g_ref, w_ref, gk_ref, km_ref, vm_ref, *, n_heads, hd):
    mn = _rms(mem_ref[...], g_ref[...]).astype(BF16)
    kv = _dot(mn, w_ref[...])
    w = n_heads * hd
    ks = []
    for hh in range(n_heads):
        ks.append(_rms(kv[:, hh * hd:(hh + 1) * hd], gk_ref[...]))
    km_ref[...] = jnp.concatenate(ks, axis=-1).astype(BF16)
    vm_ref[...] = kv[:, w:].astype(BF16)


def _memkv_call(mem, g_mem, w_mem_kv, g_k_mem, n_heads, hd):
    depth = g_mem.shape[0]
    bsz, n_mem, d = mem.shape
    w = n_heads * hd
    out = jax.ShapeDtypeStruct((depth, bsz, n_mem, w), BF16)
    return pl.pallas_call(
        functools.partial(_memkv_kernel, n_heads=n_heads, hd=hd),
        grid=(depth, bsz),
        in_specs=[
            pl.BlockSpec((None, n_mem, d), lambda l, b: (b, 0, 0)),
            pl.BlockSpec((None, 1, d), lambda l, b: (l, 0, 0)),
            pl.BlockSpec((None, d, 2 * w), lambda l, b: (l, 0, 0)),
            pl.BlockSpec((None, 1, hd), lambda l, b: (l, 0, 0)),
        ],
        out_specs=[
            pl.BlockSpec((None, None, n_mem, w), lambda l, b: (l, b, 0, 0)),
            pl.BlockSpec((None, None, n_mem, w), lambda l, b: (l, b, 0, 0)),
        ],
        out_shape=[out, out],
        compiler_params=_cparams(("arbitrary", "arbitrary")),
        name="mem_kv",
    )(mem, g_mem.reshape(depth, 1, d), w_mem_kv.astype(BF16), g_k_mem.reshape(depth, 1, hd))


def _rglru_kernel(x_ref, g_ref, win_ref, wc_ref, bc_ref, wbd_ref, br_ref, bi_ref, lam_ref, o_ref,
                  pad_ref, xprev_ref, hprev_ref, *, bd_starts):
    @pl.when(pl.program_id(0) == 0)
    def _():
        xprev_ref[...] = jnp.zeros_like(xprev_ref)
        hprev_ref[...] = jnp.zeros_like(hprev_ref)

    nb, tt, _ = x_ref.shape
    d_rnn = wc_ref.shape[1]
    x = _to_time_major(x_ref, pad_ref)
    n = x.shape[0]
    h = _rms(x, g_ref[...]).astype(BF16)
    proj = _dot(h, win_ref[...])
    xr = proj[:, :d_rnn]
    yg = proj[:, d_rnn:]

    conv = _causal_conv_tm(xr, xprev_ref[...], wc_ref, bc_ref, nb)
    xprev_ref[...] = xr[n - xprev_ref.shape[0]:, :]

    cb = conv.astype(BF16)
    win = wbd_ref.shape[1]
    ris = [_dot(cb[:, ws:ws + win], wbd_ref[j]) for j, ws in enumerate(bd_starts)]
    r = _sigmoid(jnp.concatenate([ri[:, :MXU_COLS] for ri in ris], axis=-1) + br_ref[...])
    ig = _sigmoid(jnp.concatenate([ri[:, MXU_COLS:] for ri in ris], axis=-1) + bi_ref[...])
    lam = lam_ref[...]
    neg_c = -RG_C * (jnp.minimum(lam, 0.0) - jnp.log1p(jnp.exp(-jnp.abs(lam))))
    neg_log_a = r * neg_c
    a = jnp.exp2(r * (neg_c * -LOG2E))
    z = jnp.tanh(neg_log_a) * (a * a + 1.0)
    u = jnp.where(z > 0.0, z * lax.rsqrt(z), 0.0) * ig * conv

    hcur = hprev_ref[...]
    hs = []
    for t in range(tt):
        hcur = a[t * nb:(t + 1) * nb] * hcur + u[t * nb:(t + 1) * nb]
        hs.append(hcur)
    hprev_ref[...] = hcur
    main = jnp.concatenate(hs, axis=0) * _gelu_tanh(yg)
    _from_time_major(main, o_ref, pad_ref)


def _rglru_call(x, g, w_in_xy, w_conv, b_conv, wbd, bd_starts, b_r, b_i, lam):
    bsz, s, d = x.shape
    d_rnn = w_conv.shape[1]
    tt = TT_RGLRU
    const = lambda *shape: pl.BlockSpec(shape, lambda i: (0,) * len(shape))
    return pl.pallas_call(
        functools.partial(_rglru_kernel, bd_starts=bd_starts),
        grid=(s // tt,),
        in_specs=[
            pl.BlockSpec((bsz, tt, d), lambda i: (0, i, 0)),
            const(1, d),
            const(*w_in_xy.shape),
            const(*w_conv.shape),
            const(1, d_rnn),
            const(*wbd.shape),
            const(1, d_rnn),
            const(1, d_rnn),
            const(1, d_rnn),
        ],
        out_specs=pl.BlockSpec((bsz, tt, d_rnn), lambda i: (0, i, 0)),
        out_shape=jax.ShapeDtypeStruct((bsz, s, d_rnn), BF16),
        scratch_shapes=[
            pltpu.VMEM((d_rnn // LANES, bsz * (tt + SUBLANES), LANES), F32),
            pltpu.VMEM(((w_conv.shape[0] - 1) * bsz, d_rnn), F32),
            pltpu.VMEM((bsz, d_rnn), F32),
        ],
        compiler_params=_cparams(("arbitrary",)),
        name="rglru_a",
    )(x, g, w_in_xy, w_conv, b_conv, wbd, b_r, b_i, lam)


def _mixout_kernel(x_ref, main_ref, g_ref, wq_ref, km_ref, vm_ref, gq_ref, wout_ref, o_ref, *, n_heads, hd):
    x = x_ref[...]
    d_main = main_ref.shape[1]
    h = _rms(x, g_ref[...]).astype(BF16)
    qm = _dot(h, wq_ref[...])
    mo = _mem_attention(qm, km_ref, vm_ref, gq_ref[...], n_heads, hd).astype(BF16)
    y = _dot(main_ref[...], wout_ref[:d_main, :]) + _dot(mo, wout_ref[d_main:, :])
    o_ref[...] = x + y


def _mixout_call(x, main, g, w_q, km, vm, gq, w_out, n_heads, hd):
    bsz, s, d = x.shape
    d_main = main.shape[2]
    n_mem, mw = km.shape[1:]
    ts = TS_MIX
    const = lambda *shape: pl.BlockSpec(shape, lambda b, i: (0,) * len(shape))
    return pl.pallas_call(
        functools.partial(_mixout_kernel, n_heads=n_heads, hd=hd),
        grid=(bsz, s // ts),
        in_specs=[
            pl.BlockSpec((None, ts, d), lambda b, i: (b, i, 0)),
            pl.BlockSpec((None, ts, d_main), lambda b, i: (b, i, 0)),
            const(1, d),
            const(*w_q.shape),
            pl.BlockSpec((None, n_mem, mw), lambda b, i: (b, 0, 0)),
            pl.BlockSpec((None, n_mem, mw), lambda b, i: (b, 0, 0)),
            const(1, hd),
            const(*w_out.shape),
        ],
        out_specs=pl.BlockSpec((None, ts, d), lambda b, i: (b, i, 0)),
        out_shape=jax.ShapeDtypeStruct(x.shape, F32),
        compiler_params=_cparams(("arbitrary", "arbitrary")),
        name="mixout_a",
    )(x, main, g, w_q, km, vm, gq, w_out)


def _ffn_kernel(x_ref, g_ref, wu_ref, wg_ref, cu_ref, cg_ref, bu_ref, bg_ref, wd_ref, o_ref,
                pad_ref, carry_ref):
    nb = x_ref.shape[0]

    @pl.when(pl.program_id(0) == 0)
    def _():
        carry_ref[...] = jnp.zeros_like(carry_ref)

    hb = _rms(_to_time_major(x_ref, pad_ref), g_ref[...]).astype(BF16)
    n = hb.shape[0]
    keep = carry_ref.shape[1]

    def branch(w_ref, c_ref, b_ref, slot, cols):
        pre = _dot(hb, w_ref[:, cols])
        out = _causal_conv_tm(pre, carry_ref[slot, :, cols], c_ref, b_ref, nb, cols)
        carry_ref[slot, :, cols] = pre[n - keep:, :]
        return out

    acts = []
    for j in range(wu_ref.shape[1] // FFN_SUB):
        cols = slice(j * FFN_SUB, (j + 1) * FFN_SUB)
        u = branch(wu_ref, cu_ref, bu_ref, 0, cols)
        gt = branch(wg_ref, cg_ref, bg_ref, 1, cols)
        acts.append((_gelu(gt) * u).astype(BF16))
    act = jnp.concatenate(acts, axis=1)
    y = jnp.concatenate([_dot(act, wd_ref[:, j * FFN_SUB:(j + 1) * FFN_SUB])
                         for j in range(wd_ref.shape[1] // FFN_SUB)], axis=1)
    _from_time_major(y, o_ref, pad_ref, res_ref=x_ref)


def _ffn_call(x, layer, g, w_up, w_dw, b_dw, w_down):
    bsz, s, d = x.shape
    d_ff = w_down.shape[1]
    tt = TT_FFN
    width = w_dw.shape[1]
    resident = dict(pipeline_mode=pl.Buffered(1))
    return pl.pallas_call(
        _ffn_kernel,
        grid=(s // tt,),
        in_specs=[
            pl.BlockSpec((bsz, tt, d), lambda i: (0, i, 0)),
            pl.BlockSpec((None, 1, d), lambda i: (layer, 0, 0)),
            pl.BlockSpec((None, d, d_ff), lambda i: (layer, 0, 0), **resident),
            pl.BlockSpec((None, d, d_ff), lambda i: (layer, 0, 1), **resident),
            pl.BlockSpec((None, width, d_ff), lambda i: (layer, 0, 0)),
            pl.BlockSpec((None, width, d_ff), lambda i: (layer, 0, 1)),
            pl.BlockSpec((None, 1, d_ff), lambda i: (layer, 0, 0)),
            pl.BlockSpec((None, 1, d_ff), lambda i: (layer, 0, 1)),
            pl.BlockSpec((None, d_ff, d), lambda i: (layer, 0, 0), **resident),
        ],
        out_specs=pl.BlockSpec((bsz, tt, d), lambda i: (0, i, 0)),
        out_shape=jax.ShapeDtypeStruct(x.shape, F32),
        scratch_shapes=[
            pltpu.VMEM((d // LANES, bsz * (tt + SUBLANES), LANES), F32),
            pltpu.VMEM((2, (width - 1) * bsz, d_ff), F32),
        ],
        compiler_params=_cparams(("arbitrary",)),
        name="conv_ffn",
    )(x, g, w_up, w_up, w_dw, w_dw, b_dw, b_dw, w_down)


def _kv_kernel(x_ref, g_ref, wk_ref, wvt_ref, gk_ref, k_ref, vt_ref, *, hd):
    h = _rms(x_ref[...], g_ref[...]).astype(BF16)
    k = _dot(h, wk_ref[...])
    k_ref[...] = _head_rms_rows(k, gk_ref[...], hd).astype(BF16)
    vt = _dot_nt(wvt_ref[...], h)
    ext = hd + BF16_ROWS
    n_blk, _, tq = vt_ref.shape
    ones = jnp.ones((BF16_ROWS, tq), BF16)
    for j in range(n_blk):
        for hh in range(vt.shape[0] // hd):
            vt_ref[j, hh * ext:hh * ext + hd, :] = vt[hh * hd:(hh + 1) * hd, j * tq:(j + 1) * tq].astype(BF16)
            vt_ref[j, hh * ext + hd:(hh + 1) * ext, :] = ones


def _kv_call(x, g, w_k, w_vt, gk_row, hd):
    bsz, s, d = x.shape
    w = w_k.shape[1]
    wx = (w // hd) * (hd + BF16_ROWS)
    ts = TS_KV
    return pl.pallas_call(
        functools.partial(_kv_kernel, hd=hd),
        grid=(bsz, s // ts),
        in_specs=[
            pl.BlockSpec((None, ts, d), lambda b, i: (b, i, 0)),
            pl.BlockSpec((1, d), lambda b, i: (0, 0)),
            pl.BlockSpec((d, w), lambda b, i: (0, 0)),
            pl.BlockSpec((w, d), lambda b, i: (0, 0)),
            pl.BlockSpec((1, w), lambda b, i: (0, 0)),
        ],
        out_specs=[
            pl.BlockSpec((None, ts, w), lambda b, i: (b, i, 0)),
            pl.BlockSpec((None, ts // TQ, wx, TQ), lambda b, i: (b, i, 0, 0)),
        ],
        out_shape=[jax.ShapeDtypeStruct((bsz, s, w), BF16),
                   jax.ShapeDtypeStruct((bsz, s // TQ, wx, TQ), BF16)],
        compiler_params=_cparams(("arbitrary", "arbitrary")),
        name="kv_proj",
    )(x, g, w_k, w_vt, gk_row)


def _bias_kernel(f_ref, o_ref):
    x = jnp.broadcast_to(f_ref[0], (CHUNK, KEY_ROWS))
    r = pltpu.roll(x, 0, 1, stride=1, stride_axis=0)
    band = r.T[CHUNK - 1:CHUNK - 1 + BAND, :]
    negs = jnp.full((CHUNK, CHUNK), NEG, F32)
    lo = jnp.concatenate([band, negs], axis=0)
    hi = jnp.concatenate([negs, band], axis=0)
    o_ref[0, 0:NEG_ROWS, :] = jnp.full((NEG_ROWS, 2 * CHUNK), NEG, F32)
    o_ref[0, NEG_ROWS:, :] = jnp.concatenate([lo, hi], axis=1)


def _bias_call(rel_bias):
    n_heads, n_rel = rel_bias.shape
    assert n_rel == MAX_REL + CHUNK and 2 * CHUNK == LANES
    f = jnp.concatenate([jnp.broadcast_to(rel_bias[:, n_rel - 1:], (n_heads, KEY_ROWS - 1 - n_rel)),
                         rel_bias[:, ::-1], rel_bias[:, :1]], axis=1) * LOG2E
    rows = NEG_ROWS + KEY_ROWS
    return pl.pallas_call(
        _bias_kernel,
        grid=(n_heads,),
        in_specs=[pl.BlockSpec((1, 1, KEY_ROWS), lambda h: (h, 0, 0))],
        out_specs=pl.BlockSpec((1, rows, LANES), lambda h: (h, 0, 0)),
        out_shape=jax.ShapeDtypeStruct((n_heads, rows, LANES), F32),
        compiler_params=_cparams(("arbitrary",)),
        name="band_bias",
    )(f.astype(F32)[:, None, :])


def _mixer_b_kernel(x_ref, g_ref, win_ref, gqa_ref, k0_ref, k1_ref, k2_ref, v0_ref, v1_ref, v2_ref,
                    pb_ref, km_ref, vm_ref, gq_ref, wout_ref, o_ref, qa_ref, s_ref, p_ref, att_ref,
                    *, att_w, att_hd, n_heads, hd):
    i = pl.program_id(1)
    x = x_ref[...]
    tq = x.shape[0]
    half = tq // 2
    assert half == LANES
    h = _rms(x, g_ref[...]).astype(BF16)
    proj = _dot(h, win_ref[...])
    q_t = proj[:, :att_w].T
    qn = []
    for hh in range(att_w // att_hd):
        q_h = q_t[hh * att_hd:(hh + 1) * att_hd]
        ms = jnp.mean(q_h * q_h, axis=0, keepdims=True)
        qn.append(q_h * lax.rsqrt(ms + EPS))
    qa_ref[...] = pltpu.bitcast((jnp.concatenate(qn, axis=0) * gqa_ref[...]).astype(BF16), jnp.uint32)
    qm = proj[:, att_w:]

    k_refs = (k0_ref, k1_ref, k2_ref)
    v_refs = (v0_ref, v1_ref, v2_ref)
    ok0 = i >= 2
    ok1 = i >= 1
    ext = att_hd + BF16_ROWS

    zblk = jnp.zeros((half, half), BF16)
    for slot in range(2):
        p_ref[slot, 0, 0:half, half:] = zblk
        p_ref[slot, 2, half:, 0:half] = zblk

    def bias_rows(hh, ok, start, size):
        first = pl.multiple_of(jnp.where(ok, NEG_ROWS + start, 0), LANES)
        return pb_ref[hh, pl.ds(first, size), :]

    def scores(hh):
        slot = hh % 2
        col = (hh // 2) * LANES
        rows = att_hd // 2
        q_t = pltpu.bitcast(qa_ref[hh * rows:(hh + 1) * rows, :], BF16)
        qz = jnp.concatenate([q_t, jnp.zeros_like(q_t)] if hh % 2 == 0 else [jnp.zeros_like(q_t), q_t], axis=0)
        st = [_dot(kr[:, col:col + LANES], qz) for kr in k_refs]
        s_ref[slot, 0, 0:tq, :] = st[0][:, :half] + bias_rows(hh, ok0, 0, tq)
        s_ref[slot, 0, tq:2 * tq, :] = st[1][:, :half] + bias_rows(hh, ok1, tq, tq)
        s_ref[slot, 0, 2 * tq:, :] = st[2][:half, :half] + pb_ref[hh, NEG_ROWS + 2 * tq:, :]
        s_ref[slot, 1, 0:half, :] = st[0][half:, half:] + bias_rows(hh, ok0, 0, half)
        s_ref[slot, 1, half:half + tq, :] = st[1][:, half:] + bias_rows(hh, ok1, half, tq)
        s_ref[slot, 1, half + tq:, :] = st[2][:, half:] + pb_ref[hh, NEG_ROWS + half + tq:, :]

    def probs(hh):
        slot = hh % 2
        z = s_ref[slot, 0]
        e = jnp.exp2(z - jnp.max(z, axis=0, keepdims=True)).astype(BF16)
        p_ref[slot, 0, :, 0:half] = e[0:tq]
        p_ref[slot, 1, :, 0:half] = e[tq:2 * tq]
        p_ref[slot, 2, 0:half, 0:half] = e[2 * tq:]
        z = s_ref[slot, 1]
        e = jnp.exp2(z - jnp.max(z, axis=0, keepdims=True)).astype(BF16)
        p_ref[slot, 0, half:, half:] = e[0:half]
        p_ref[slot, 1, :, half:] = e[half:half + tq]
        p_ref[slot, 2, :, half:] = e[half + tq:]

    def values(hh):
        slot = hh % 2
        vt_h = jnp.concatenate([vr[hh * ext:(hh + 1) * ext, :] for vr in v_refs], axis=1)
        ot = _dot(vt_h, p_ref[slot].reshape(len(v_refs) * tq, tq))
        att_ref[hh * att_hd:(hh + 1) * att_hd, :] = ot[:att_hd] / ot[att_hd:att_hd + 1]

    n_att = att_w // att_hd
    for step in range(n_att + 2):
        if 0 <= step - 2:
            values(step - 2)
        if 0 <= step - 1 < n_att:
            probs(step - 1)
        if step < n_att:
            scores(step)

    att = att_ref[...].T.astype(BF16)
    mo = _mem_attention(qm, km_ref, vm_ref, gq_ref[...], n_heads, hd).astype(BF16)
    y = _dot(att, wout_ref[:att_w, :]) + _dot(mo, wout_ref[att_w:, :])
    o_ref[...] = x + y


def _mixer_b_call(x, g, w_in, gqa_row, k, vt, pb, km, vm, gq, w_out, att_hd, n_heads, hd):
    bsz, s, d = x.shape
    att_w = k.shape[2]
    vt_w = vt.shape[2]
    n_mem, mw = km.shape[1:]
    tq = TQ
    const = lambda *shape: pl.BlockSpec(shape, lambda b, i: (0,) * len(shape))
    kspec = lambda r: pl.BlockSpec((None, tq, att_w), lambda b, i: (b, jnp.maximum(i - 2 + r, 0), 0))
    vspec = lambda r: pl.BlockSpec((None, None, vt_w, tq), lambda b, i: (b, jnp.maximum(i - 2 + r, 0), 0, 0))
    return pl.pallas_call(
        functools.partial(_mixer_b_kernel, att_w=att_w, att_hd=att_hd, n_heads=n_heads, hd=hd),
        grid=(bsz, s // tq),
        in_specs=[
            pl.BlockSpec((None, tq, d), lambda b, i: (b, i, 0)),
            const(1, d),
            const(*w_in.shape),
            const(att_w, tq),
            kspec(0), kspec(1), kspec(2),
            vspec(0), vspec(1), vspec(2),
            const(*pb.shape),
            pl.BlockSpec((None, n_mem, mw), lambda b, i: (b, 0, 0)),
            pl.BlockSpec((None, n_mem, mw), lambda b, i: (b, 0, 0)),
            const(1, hd),
            const(*w_out.shape),
        ],
        out_specs=pl.BlockSpec((None, tq, d), lambda b, i: (b, i, 0)),
        out_shape=jax.ShapeDtypeStruct(x.shape, F32),
        scratch_shapes=[
            pltpu.VMEM((att_w // 2, tq), jnp.uint32),
            pltpu.VMEM((2, 2, KEY_ROWS, LANES), F32),
            pltpu.VMEM((2, 3, tq, tq), BF16),
            pltpu.VMEM((att_w, tq), F32),
        ],
        compiler_params=_cparams(("arbitrary", "arbitrary")),
        name="mixer_b",
    )(x, g, w_in, gqa_row, k, k, k, vt, vt, vt, pb, km, vm, gq, w_out)


def _block_diag_windows(w_r, w_i):
    n, bw, _ = w_r.shape
    width = n * bw
    assert width % MXU_COLS == 0
    reps = -(-MXU_COLS // bw) + 1
    starts, tiles = [], []
    for j in range(width // MXU_COLS):
        c0 = j * MXU_COLS
        lo = (c0 // bw) * bw
        hi = -((-(c0 + MXU_COLS)) // bw) * bw
        ws = min((lo // LANES) * LANES, width - BD_WINDOW)
        assert ws <= lo and hi <= ws + BD_WINDOW
        rows = ws + jnp.arange(BD_WINDOW)[:, None]
        cols = c0 + jnp.arange(MXU_COLS)[None, :]
        on_diag = (rows // bw) == (cols // bw)

        def window(w):
            rep = jnp.tile(w.reshape(width, bw)[ws:ws + BD_WINDOW], (1, reps))[:, c0 % bw:c0 % bw + MXU_COLS]
            return jnp.where(on_diag, rep, 0.0)

        tiles.append(jnp.concatenate([window(w_r), window(w_i)], axis=1))
        starts.append(ws)
    return jnp.stack(tiles).astype(BF16), tuple(starts)


def kernel(x, mem, g_mix, g_mem, w_mem_kv, g_q_mem, g_k_mem, w_in_a, w_conv_a, b_conv_a, w_r_a, b_r_a,
           w_i_a, b_i_a, lam_a, w_out_a, g_kv, w_kv, g_k_att, w_in_b, g_q_att, rel_bias_b, w_out_b,
           g_ffn, w_up, w_dw_ffn, b_dw_ffn, w_down):
    bsz, s, d = x.shape
    depth = g_mix.shape[0]
    n_a = w_in_a.shape[0]
    mem_hd = g_q_mem.shape[1]
    mem_w = w_mem_kv.shape[2] // 2
    mem_heads = mem_w // mem_hd
    att_hd = g_k_att.shape[0]
    att_w = w_kv.shape[1] // 2
    att_heads = att_w // att_hd
    d_rnn = w_conv_a.shape[2]
    assert att_hd * 2 == LANES and att_hd == CHUNK and bsz == SUBLANES
    assert s % TS_KV == 0 and s % TS_MIX == 0 and s % TQ == 0 and s % TT_FFN == 0 and s % TT_RGLRU == 0

    row = lambda v: v.reshape(1, -1).astype(F32)
    km_all, vm_all = _memkv_call(mem, g_mem, w_mem_kv, g_k_mem, mem_heads, mem_hd)

    g_ffn3 = g_ffn[:, None, :].astype(F32)
    b_dw3 = b_dw_ffn[:, None, :].astype(F32)
    w_up_bf = w_up.astype(BF16)
    w_down_bf = w_down.astype(BF16)
    k = vt = None
    for l in range(depth):
        if l < n_a:
            wbd, bd_starts = _block_diag_windows(w_r_a[l], w_i_a[l])
            w_in = w_in_a[l].astype(BF16)
            main = _rglru_call(x, row(g_mix[l]), w_in[:, :2 * d_rnn], w_conv_a[l], row(b_conv_a[l]), wbd,
                               bd_starts, row(b_r_a[l]), row(b_i_a[l]), row(lam_a[l]))
            x = _mixout_call(x, main, row(g_mix[l]), w_in[:, 2 * d_rnn:], km_all[l], vm_all[l],
                             row(g_q_mem[l]), w_out_a[l].astype(BF16), mem_heads, mem_hd)
        else:
            j = l - n_a
            if j == 0:
                k, vt = _kv_call(x, row(g_kv), w_kv[:, :att_w].astype(BF16), w_kv[:, att_w:].T.astype(BF16),
                                 row(jnp.tile(g_k_att, att_heads)), att_hd)
            gqa = jnp.broadcast_to((jnp.tile(g_q_att[j], att_heads) * (att_hd ** -0.5 * LOG2E))[:, None],
                                   (att_w, TQ)).astype(F32)
            x = _mixer_b_call(x, row(g_mix[l]), w_in_b[j].astype(BF16), gqa, k, vt, _bias_call(rel_bias_b[j]),
                              km_all[l], vm_all[l], row(g_q_mem[l]), w_out_b[j].astype(BF16),
                              att_hd, mem_heads, mem_hd)
        x = _ffn_call(x, l, g_ffn3, w_up_bf, w_dw_ffn, b_dw3, w_down_bf)
    return x
```

```python
import functools
import math

import jax
import jax.numpy as jnp
from jax import lax
from jax.experimental import pallas as pl
from jax.experimental.pallas import tpu as pltpu

EPS = 1e-6
CHUNK = 64
N_PREV_CHUNKS = 8
BAND = (N_PREV_CHUNKS + 1) * CHUNK
MAX_REL = 256
RG_C = 8.0
NEG = -1e30
LOG2E = math.log2(math.e)

LANES = 128
SUBLANES = 8
BF16_ROWS = 16
MXU_COLS = 256
BD_WINDOW = 512
VMEM_LIMIT = 56 * 1024 * 1024

TT_RGLRU = 32
TT_FFN = 64
FFN_SUB = 256
TS_MIX = 1024
TS_KV = 1024
TQ = 4 * CHUNK
KEY_ROWS = BAND + CHUNK
NEG_ROWS = TQ

F32 = jnp.float32
BF16 = jnp.bfloat16


def _cparams(sem):
    return pltpu.CompilerParams(dimension_semantics=sem, vmem_limit_bytes=VMEM_LIMIT)


def _dot(a, b):
    return jnp.dot(a, b, preferred_element_type=F32)


def _dot_nt(a, b):
    return lax.dot_general(a, b, (((1,), (1,)), ((), ())), preferred_element_type=F32)


def _rms(x, g):
    ms = jnp.mean(x * x, axis=-1, keepdims=True)
    return x * lax.rsqrt(ms + EPS) * g


def _gelu(x):
    c = -2.0 * 0.7978845608028654 * LOG2E
    return x / (1.0 + jnp.exp2(x * (c + (c * 0.044715) * (x * x))))


def _gelu_tanh(x):
    c = 0.7978845608028654
    return x * (0.5 + 0.5 * jnp.tanh(x * (c + (c * 0.044715) * (x * x))))


def _sigmoid(x):
    return 0.5 + 0.5 * jnp.tanh(0.5 * x)


def _to_time_major(x_ref, pad_ref):
    nb, tt, d = x_ref.shape
    pitch = pad_ref.shape[1] // nb
    for b in range(nb):
        for c in range(d // LANES):
            pad_ref[c, b * pitch:b * pitch + tt, :] = x_ref[b, :, c * LANES:(c + 1) * LANES].astype(F32)
    rows = []
    for t in range(tt):
        rows.append(jnp.concatenate(
            [pad_ref[c, pl.ds(t, nb, stride=pitch), :] for c in range(d // LANES)], axis=1))
    return jnp.concatenate(rows, axis=0)


def _from_time_major(y, o_ref, pad_ref, res_ref=None):
    nb, tt, d = o_ref.shape
    pitch = pad_ref.shape[1] // nb
    for t in range(tt):
        for c in range(d // LANES):
            pad_ref[c, pl.ds(t, nb, stride=pitch), :] = y[t * nb:(t + 1) * nb, c * LANES:(c + 1) * LANES]
    for b in range(nb):
        for c in range(d // LANES):
            v = pad_ref[c, b * pitch:b * pitch + tt, :]
            if res_ref is not None:
                v = v + res_ref[b, :, c * LANES:(c + 1) * LANES]
            o_ref[b, :, c * LANES:(c + 1) * LANES] = v.astype(o_ref.dtype)


def _causal_conv_tm(x, prev, w_ref, b_ref, nb, cols=slice(None)):
    n = x.shape[0]
    width = w_ref.shape[0]
    xc = jnp.concatenate([prev, x], axis=0)
    out = x * w_ref[width - 1:width, cols] + b_ref[:, cols]
    for k in range(width - 1):
        out = out + xc[k * nb:k * nb + n] * w_ref[k:k + 1, cols]
    return out


def _head_rms_rows(x, g, hd):
    t, w = x.shape
    lane = lax.broadcasted_iota(jnp.int32, (t, LANES), 1)
    lo = lane < hd
    outs = []
    for p in range(w // LANES):
        xp = x[:, p * LANES:(p + 1) * LANES]
        x2 = xp * xp
        s_lo = jnp.sum(jnp.where(lo, x2, 0.0), axis=-1, keepdims=True)
        s_hi = jnp.sum(jnp.where(lo, 0.0, x2), axis=-1, keepdims=True)
        r_lo = lax.rsqrt(s_lo * (1.0 / hd) + EPS)
        r_hi = lax.rsqrt(s_hi * (1.0 / hd) + EPS)
        outs.append(xp * jnp.where(lo, r_lo, r_hi))
    return jnp.concatenate(outs, axis=-1) * g


def _mem_attention(qm, km_ref, vm_ref, gq, n_heads, hd):
    outs = []
    for hh in range(n_heads):
        sl = slice(hh * hd, (hh + 1) * hd)
        qn = (_rms(qm[:, sl], gq) * (hd ** -0.5)).astype(BF16)
        s = _dot_nt(qn, km_ref[:, sl])
        m = jnp.max(s, axis=-1, keepdims=True)
        e = jnp.exp(s - m)
        l = jnp.sum(e, axis=-1, keepdims=True)
        o = _dot(e.astype(BF16), vm_ref[:, sl])
        outs.append(o / l)
    return jnp.concatenate(outs, axis=-1)


def _memkv_kernel(mem_ref, g_ref, w_ref, gk_ref, km_ref, vm_ref, *, n_heads, hd):
    bsz, n_mem, d = mem_ref.shape
    mn = _rms(mem_ref[...].reshape(bsz * n_mem, d), g_ref[...]).astype(BF16)
    kv = _dot(mn, w_ref[...])
    w = n_heads * hd
    km = jnp.concatenate([_rms(kv[:, hh * hd:(hh + 1) * hd], gk_ref[...]) for hh in range(n_heads)], axis=-1)
    km_ref[...] = km.astype(BF16).reshape(bsz, n_mem, w)
    vm_ref[...] = kv[:, w:].astype(BF16).reshape(bsz, n_mem, w)


def _memkv_call(mem, g_mem, w_mem_kv, g_k_mem, n_heads, hd):
    depth = g_mem.shape[0]
    bsz, n_mem, d = mem.shape
    w = n_heads * hd
    out = jax.ShapeDtypeStruct((depth, bsz, n_mem, w), BF16)
    return pl.pallas_call(
        functools.partial(_memkv_kernel, n_heads=n_heads, hd=hd),
        grid=(depth,),
        in_specs=[
            pl.BlockSpec((bsz, n_mem, d), lambda l: (0, 0, 0)),
            pl.BlockSpec((None, 1, d), lambda l: (l, 0, 0)),
            pl.BlockSpec((None, d, 2 * w), lambda l: (l, 0, 0)),
            pl.BlockSpec((None, 1, hd), lambda l: (l, 0, 0)),
        ],
        out_specs=[
            pl.BlockSpec((None, bsz, n_mem, w), lambda l: (l, 0, 0, 0)),
            pl.BlockSpec((None, bsz, n_mem, w), lambda l: (l, 0, 0, 0)),
        ],
        out_shape=[out, out],
        compiler_params=_cparams(("arbitrary",)),
        name="mem_kv",
    )(mem, g_mem.reshape(depth, 1, d), w_mem_kv.astype(BF16), g_k_mem.reshape(depth, 1, hd))


def _rglru_kernel(x_ref, g_ref, win_ref, wc_ref, bc_ref, wbd_ref, br_ref, bi_ref, lam_ref, o_ref,
                  pad_ref, pado_ref, xprev_ref, hprev_ref, *, bd_starts):
    @pl.when(pl.program_id(0) == 0)
    def _():
        xprev_ref[...] = jnp.zeros_like(xprev_ref)
        hprev_ref[...] = jnp.zeros_like(hprev_ref)

    nb, tt, _ = x_ref.shape
    d_rnn = wc_ref.shape[1]
    x = _to_time_major(x_ref, pad_ref)
    n = x.shape[0]
    h = _rms(x, g_ref[...]).astype(BF16)
    proj = _dot(h, win_ref[...])
    xr = proj[:, :d_rnn]
    yg = proj[:, d_rnn:]

    conv = _causal_conv_tm(xr, xprev_ref[...], wc_ref, bc_ref, nb)
    xprev_ref[...] = xr[n - xprev_ref.shape[0]:, :]

    cb = conv.astype(BF16)
    win = wbd_ref.shape[1]
    ris = [_dot(cb[:, ws:ws + win], wbd_ref[j]) for j, ws in enumerate(bd_starts)]
    r = _sigmoid(jnp.concatenate([ri[:, :MXU_COLS] for ri in ris], axis=-1) + br_ref[...])
    ig = _sigmoid(jnp.concatenate([ri[:, MXU_COLS:] for ri in ris], axis=-1) + bi_ref[...])
    lam = lam_ref[...]
    neg_c = -RG_C * (jnp.minimum(lam, 0.0) - jnp.log1p(jnp.exp(-jnp.abs(lam))))
    neg_log_a = r * neg_c
    a = jnp.exp2(r * (neg_c * -LOG2E))
    z = jnp.tanh(neg_log_a) * (a * a + 1.0)
    u = jnp.where(z > 0.0, z * lax.rsqrt(z), 0.0) * ig * conv

    hcur = hprev_ref[...]
    hs = []
    for t in range(tt):
        hcur = a[t * nb:(t + 1) * nb] * hcur + u[t * nb:(t + 1) * nb]
        hs.append(hcur)
    hprev_ref[...] = hcur
    main = jnp.concatenate(hs, axis=0) * _gelu_tanh(yg)
    _from_time_major(main, o_ref, pado_ref)


def _rglru_call(x, g, w_in_xy, w_conv, b_conv, wbd, bd_starts, b_r, b_i, lam):
    bsz, s, d = x.shape
    d_rnn = w_conv.shape[1]
    tt = TT_RGLRU
    const = lambda *shape: pl.BlockSpec(shape, lambda i: (0,) * len(shape))
    return pl.pallas_call(
        functools.partial(_rglru_kernel, bd_starts=bd_starts),
        grid=(s // tt,),
        in_specs=[
            pl.BlockSpec((bsz, tt, d), lambda i: (0, i, 0)),
            const(1, d),
            const(*w_in_xy.shape),
            const(*w_conv.shape),
            const(1, d_rnn),
            const(*wbd.shape),
            const(1, d_rnn),
            const(1, d_rnn),
            const(1, d_rnn),
        ],
        out_specs=pl.BlockSpec((bsz, tt, d_rnn), lambda i: (0, i, 0)),
        out_shape=jax.ShapeDtypeStruct((bsz, s, d_rnn), BF16),
        scratch_shapes=[
            pltpu.VMEM((d // LANES, bsz * (tt + SUBLANES), LANES), F32),
            pltpu.VMEM((d_rnn // LANES, bsz * (tt + SUBLANES // 2), LANES), F32),
            pltpu.VMEM(((w_conv.shape[0] - 1) * bsz, d_rnn), F32),
            pltpu.VMEM((bsz, d_rnn), F32),
        ],
        compiler_params=_cparams(("arbitrary",)),
        name="rglru_a",
    )(x, g, w_in_xy, w_conv, b_conv, wbd, b_r, b_i, lam)


def _mixout_kernel(x_ref, main_ref, g_ref, wq_ref, km_ref, vm_ref, gq_ref, wout_ref, o_ref, *, n_heads, hd):
    x = x_ref[...]
    d_main = main_ref.shape[1]
    h = _rms(x, g_ref[...]).astype(BF16)
    qm = _dot(h, wq_ref[...])
    mo = _mem_attention(qm, km_ref, vm_ref, gq_ref[...], n_heads, hd).astype(BF16)
    y = _dot(main_ref[...], wout_ref[:d_main, :]) + _dot(mo, wout_ref[d_main:, :])
    o_ref[...] = x + y


def _mixout_call(x, main, g, w_q, km, vm, gq, w_out, n_heads, hd):
    bsz, s, d = x.shape
    d_main = main.shape[2]
    n_mem, mw = km.shape[1:]
    ts = TS_MIX
    const = lambda *shape: pl.BlockSpec(shape, lambda b, i: (0,) * len(shape))
    return pl.pallas_call(
        functools.partial(_mixout_kernel, n_heads=n_heads, hd=hd),
        grid=(bsz, s // ts),
        in_specs=[
            pl.BlockSpec((None, ts, d), lambda b, i: (b, i, 0)),
            pl.BlockSpec((None, ts, d_main), lambda b, i: (b, i, 0)),
            const(1, d),
            const(*w_q.shape),
            pl.BlockSpec((None, n_mem, mw), lambda b, i: (b, 0, 0)),
            pl.BlockSpec((None, n_mem, mw), lambda b, i: (b, 0, 0)),
            const(1, hd),
            const(*w_out.shape),
        ],
        out_specs=pl.BlockSpec((None, ts, d), lambda b, i: (b, i, 0)),
        out_shape=jax.ShapeDtypeStruct(x.shape, F32),
        compiler_params=_cparams(("arbitrary", "arbitrary")),
        name="mixout_a",
    )(x, main, g, w_q, km, vm, gq, w_out)


def _ffn_kernel(x_ref, g_ref, wu_ref, wg_ref, cu_ref, cg_ref, bu_ref, bg_ref, wd_ref, o_ref,
                pad_ref, pado_ref, carry_ref):
    nb = x_ref.shape[0]

    @pl.when(pl.program_id(0) == 0)
    def _():
        carry_ref[...] = jnp.zeros_like(carry_ref)

    hb = _rms(_to_time_major(x_ref, pad_ref), g_ref[...]).astype(BF16)
    n = hb.shape[0]
    keep = carry_ref.shape[1]

    def branch(w_ref, c_ref, b_ref, slot, cols):
        pre = _dot(hb, w_ref[:, cols])
        out = _causal_conv_tm(pre, carry_ref[slot, :, cols], c_ref, b_ref, nb, cols)
        carry_ref[slot, :, cols] = pre[n - keep:, :]
        return out

    acts = []
    for j in range(wu_ref.shape[1] // FFN_SUB):
        cols = slice(j * FFN_SUB, (j + 1) * FFN_SUB)
        u = branch(wu_ref, cu_ref, bu_ref, 0, cols)
        gt = branch(wg_ref, cg_ref, bg_ref, 1, cols)
        acts.append((_gelu(gt) * u).astype(BF16))
    act = jnp.concatenate(acts, axis=1)
    y = jnp.concatenate([_dot(act, wd_ref[:, j * FFN_SUB:(j + 1) * FFN_SUB])
                         for j in range(wd_ref.shape[1] // FFN_SUB)], axis=1)
    _from_time_major(y, o_ref, pado_ref, res_ref=x_ref)


def _ffn_call(x, layer, g, w_up, w_dw, b_dw, w_down):
    bsz, s, d = x.shape
    d_ff = w_down.shape[1]
    tt = TT_FFN
    width = w_dw.shape[1]
    resident = dict(pipeline_mode=pl.Buffered(1))
    return pl.pallas_call(
        _ffn_kernel,
        grid=(s // tt,),
        in_specs=[
            pl.BlockSpec((bsz, tt, d), lambda i: (0, i, 0)),
            pl.BlockSpec((None, 1, d), lambda i: (layer, 0, 0)),
            pl.BlockSpec((None, d, d_ff), lambda i: (layer, 0, 0), **resident),
            pl.BlockSpec((None, d, d_ff), lambda i: (layer, 0, 1), **resident),
            pl.BlockSpec((None, width, d_ff), lambda i: (layer, 0, 0)),
            pl.BlockSpec((None, width, d_ff), lambda i: (layer, 0, 1)),
            pl.BlockSpec((None, 1, d_ff), lambda i: (layer, 0, 0)),
            pl.BlockSpec((None, 1, d_ff), lambda i: (layer, 0, 1)),
            pl.BlockSpec((None, d_ff, d), lambda i: (layer, 0, 0), **resident),
        ],
        out_specs=pl.BlockSpec((bsz, tt, d), lambda i: (0, i, 0)),
        out_shape=jax.ShapeDtypeStruct(x.shape, F32),
        scratch_shapes=[
            pltpu.VMEM((d // LANES, bsz * (tt + SUBLANES), LANES), F32),
            pltpu.VMEM((d // LANES, bsz * (tt + SUBLANES // 2), LANES), F32),
            pltpu.VMEM((2, (width - 1) * bsz, d_ff), F32),
        ],
        compiler_params=_cparams(("arbitrary",)),
        name="conv_ffn",
    )(x, g, w_up, w_up, w_dw, w_dw, b_dw, b_dw, w_down)


def _kv_kernel(x_ref, g_ref, wk_ref, wvt_ref, gk_ref, k_ref, vt_ref, *, hd):
    h = _rms(x_ref[...], g_ref[...]).astype(BF16)
    k = _dot(h, wk_ref[...])
    k_ref[...] = _head_rms_rows(k, gk_ref[...], hd).astype(BF16)
    vt = _dot_nt(wvt_ref[...], h)
    ext = hd + BF16_ROWS
    n_blk, _, tq = vt_ref.shape
    ones = jnp.ones((BF16_ROWS, tq), BF16)
    for j in range(n_blk):
        for hh in range(vt.shape[0] // hd):
            vt_ref[j, hh * ext:hh * ext + hd, :] = vt[hh * hd:(hh + 1) * hd, j * tq:(j + 1) * tq].astype(BF16)
            vt_ref[j, hh * ext + hd:(hh + 1) * ext, :] = ones


def _kv_call(x, g, w_k, w_vt, gk_row, hd):
    bsz, s, d = x.shape
    w = w_k.shape[1]
    wx = (w // hd) * (hd + BF16_ROWS)
    ts = TS_KV
    return pl.pallas_call(
        functools.partial(_kv_kernel, hd=hd),
        grid=(bsz, s // ts),
        in_specs=[
            pl.BlockSpec((None, ts, d), lambda b, i: (b, i, 0)),
            pl.BlockSpec((1, d), lambda b, i: (0, 0)),
            pl.BlockSpec((d, w), lambda b, i: (0, 0)),
            pl.BlockSpec((w, d), lambda b, i: (0, 0)),
            pl.BlockSpec((1, w), lambda b, i: (0, 0)),
        ],
        out_specs=[
            pl.BlockSpec((None, ts, w), lambda b, i: (b, i, 0)),
            pl.BlockSpec((None, ts // TQ, wx, TQ), lambda b, i: (b, i, 0, 0)),
        ],
        out_shape=[jax.ShapeDtypeStruct((bsz, s, w), BF16),
                   jax.ShapeDtypeStruct((bsz, s // TQ, wx, TQ), BF16)],
        compiler_params=_cparams(("arbitrary", "arbitrary")),
        name="kv_proj",
    )(x, g, w_k, w_vt, gk_row)


def _bias_kernel(f_ref, o_ref):
    x = jnp.broadcast_to(f_ref[0], (CHUNK, KEY_ROWS))
    r = pltpu.roll(x, 0, 1, stride=1, stride_axis=0)
    band = r.T[CHUNK - 1:CHUNK - 1 + BAND, :]
    negs = jnp.full((CHUNK, CHUNK), NEG, F32)
    lo = jnp.concatenate([band, negs], axis=0)
    hi = jnp.concatenate([negs, band], axis=0)
    o_ref[0, 0:NEG_ROWS, :] = jnp.full((NEG_ROWS, 2 * CHUNK), NEG, F32)
    o_ref[0, NEG_ROWS:, :] = jnp.concatenate([lo, hi], axis=1)


def _bias_call(rel_bias):
    n_heads, n_rel = rel_bias.shape
    assert n_rel == MAX_REL + CHUNK and 2 * CHUNK == LANES
    f = jnp.concatenate([jnp.broadcast_to(rel_bias[:, n_rel - 1:], (n_heads, KEY_ROWS - 1 - n_rel)),
                         rel_bias[:, ::-1], rel_bias[:, :1]], axis=1) * LOG2E
    rows = NEG_ROWS + KEY_ROWS
    return pl.pallas_call(
        _bias_kernel,
        grid=(n_heads,),
        in_specs=[pl.BlockSpec((1, 1, KEY_ROWS), lambda h: (h, 0, 0))],
        out_specs=pl.BlockSpec((1, rows, LANES), lambda h: (h, 0, 0)),
        out_shape=jax.ShapeDtypeStruct((n_heads, rows, LANES), F32),
        compiler_params=_cparams(("arbitrary",)),
        name="band_bias",
    )(f.astype(F32)[:, None, :])


def _mixer_b_kernel(x_ref, g_ref, win_ref, gqa_ref, k0_ref, k1_ref, k2_ref, v0_ref, v1_ref, v2_ref,
                    pb_ref, km_ref, vm_ref, gq_ref, wout_ref, o_ref, qa_ref, s_ref, p_ref, att_ref,
                    *, att_w, att_hd, n_heads, hd):
    i = pl.program_id(1)
    x = x_ref[...]
    tq = x.shape[0]
    half = tq // 2
    assert half == LANES
    h = _rms(x, g_ref[...]).astype(BF16)
    proj = _dot(h, win_ref[...])
    q_t = proj[:, :att_w].T
    qn = []
    for hh in range(att_w // att_hd):
        q_h = q_t[hh * att_hd:(hh + 1) * att_hd]
        ms = jnp.mean(q_h * q_h, axis=0, keepdims=True)
        qn.append(q_h * lax.rsqrt(ms + EPS))
    qa_ref[...] = pltpu.bitcast((jnp.concatenate(qn, axis=0) * gqa_ref[...]).astype(BF16), jnp.uint32)
    qm = proj[:, att_w:]

    k_refs = (k0_ref, k1_ref, k2_ref)
    v_refs = (v0_ref, v1_ref, v2_ref)
    ok0 = i >= 2
    ok1 = i >= 1
    ext = att_hd + BF16_ROWS

    zblk = jnp.zeros((half, half), BF16)
    for slot in range(2):
        p_ref[slot, 0, 0:half, half:] = zblk
        p_ref[slot, 2, half:, 0:half] = zblk

    def bias_rows(hh, ok, start, size):
        first = pl.multiple_of(jnp.where(ok, NEG_ROWS + start, 0), LANES)
        return pb_ref[hh, pl.ds(first, size), :]

    def scores(hh):
        slot = hh % 2
        col = (hh // 2) * LANES
        rows = att_hd // 2
        q_t = pltpu.bitcast(qa_ref[hh * rows:(hh + 1) * rows, :], BF16)
        qz = jnp.concatenate([q_t, jnp.zeros_like(q_t)] if hh % 2 == 0 else [jnp.zeros_like(q_t), q_t], axis=0)
        st = [_dot(kr[:, col:col + LANES], qz) for kr in k_refs]
        s_ref[slot, 0, 0:tq, :] = st[0][:, :half] + bias_rows(hh, ok0, 0, tq)
        s_ref[slot, 0, tq:2 * tq, :] = st[1][:, :half] + bias_rows(hh, ok1, tq, tq)
        s_ref[slot, 0, 2 * tq:, :] = st[2][:half, :half] + pb_ref[hh, NEG_ROWS + 2 * tq:, :]
        s_ref[slot, 1, 0:half, :] = st[0][half:, half:] + bias_rows(hh, ok0, 0, half)
        s_ref[slot, 1, half:half + tq, :] = st[1][:, half:] + bias_rows(hh, ok1, half, tq)
        s_ref[slot, 1, half + tq:, :] = st[2][:, half:] + pb_ref[hh, NEG_ROWS + half + tq:, :]

    def probs(hh):
        slot = hh % 2
        z = s_ref[slot, 0]
        e = jnp.exp2(z - jnp.max(z, axis=0, keepdims=True)).astype(BF16)
        p_ref[slot, 0, :, 0:half] = e[0:tq]
        p_ref[slot, 1, :, 0:half] = e[tq:2 * tq]
        p_ref[slot, 2, 0:half, 0:half] = e[2 * tq:]
        z = s_ref[slot, 1]
        e = jnp.exp2(z - jnp.max(z, axis=0, keepdims=True)).astype(BF16)
        p_ref[slot, 0, half:, half:] = e[0:half]
        p_ref[slot, 1, :, half:] = e[half:half + tq]
        p_ref[slot, 2, :, half:] = e[half + tq:]

    def values(hh):
        slot = hh % 2
        vt_h = jnp.concatenate([vr[hh * ext:(hh + 1) * ext, :] for vr in v_refs], axis=1)
        ot = _dot(vt_h, p_ref[slot].reshape(len(v_refs) * tq, tq))
        att_ref[hh * att_hd:(hh + 1) * att_hd, :] = ot[:att_hd] / ot[att_hd:att_hd + 1]

    n_att = att_w // att_hd
    for step in range(n_att + 2):
        if 0 <= step - 2:
            values(step - 2)
        if 0 <= step - 1 < n_att:
            probs(step - 1)
        if step < n_att:
            scores(step)

    att = att_ref[...].T.astype(BF16)
    mo = _mem_attention(qm, km_ref, vm_ref, gq_ref[...], n_heads, hd).astype(BF16)
    y = _dot(att, wout_ref[:att_w, :]) + _dot(mo, wout_ref[att_w:, :])
    o_ref[...] = x + y


def _mixer_b_call(x, g, w_in, gqa_row, k, vt, pb, km, vm, gq, w_out, att_hd, n_heads, hd):
    bsz, s, d = x.shape
    att_w = k.shape[2]
    vt_w = vt.shape[2]
    n_mem, mw = km.shape[1:]
    tq = TQ
    const = lambda *shape: pl.BlockSpec(shape, lambda b, i: (0,) * len(shape))
    kspec = lambda r: pl.BlockSpec((None, tq, att_w), lambda b, i: (b, jnp.maximum(i - 2 + r, 0), 0))
    vspec = lambda r: pl.BlockSpec((None, None, vt_w, tq), lambda b, i: (b, jnp.maximum(i - 2 + r, 0), 0, 0))
    return pl.pallas_call(
        functools.partial(_mixer_b_kernel, att_w=att_w, att_hd=att_hd, n_heads=n_heads, hd=hd),
        grid=(bsz, s // tq),
        in_specs=[
            pl.BlockSpec((None, tq, d), lambda b, i: (b, i, 0)),
            const(1, d),
            const(*w_in.shape),
            const(att_w, tq),
            kspec(0), kspec(1), kspec(2),
            vspec(0), vspec(1), vspec(2),
            const(*pb.shape),
            pl.BlockSpec((None, n_mem, mw), lambda b, i: (b, 0, 0)),
            pl.BlockSpec((None, n_mem, mw), lambda b, i: (b, 0, 0)),
            const(1, hd),
            const(*w_out.shape),
        ],
        out_specs=pl.BlockSpec((None, tq, d), lambda b, i: (b, i, 0)),
        out_shape=jax.ShapeDtypeStruct(x.shape, F32),
        scratch_shapes=[
            pltpu.VMEM((att_w // 2, tq), jnp.uint32),
            pltpu.VMEM((2, 2, KEY_ROWS, LANES), F32),
            pltpu.VMEM((2, 3, tq, tq), BF16),
            pltpu.VMEM((att_w, tq), F32),
        ],
        compiler_params=_cparams(("arbitrary", "arbitrary")),
        name="mixer_b",
    )(x, g, w_in, gqa_row, k, k, k, vt, vt, vt, pb, km, vm, gq, w_out)


def _block_diag_windows(w_r, w_i):
    n, bw, _ = w_r.shape
    width = n * bw
    assert width % MXU_COLS == 0
    reps = -(-MXU_COLS // bw) + 1
    starts, tiles = [], []
    for j in range(width // MXU_COLS):
        c0 = j * MXU_COLS
        lo = (c0 // bw) * bw
        hi = -((-(c0 + MXU_COLS)) // bw) * bw
        ws = min((lo // LANES) * LANES, width - BD_WINDOW)
        assert ws <= lo and hi <= ws + BD_WINDOW
        rows = ws + jnp.arange(BD_WINDOW)[:, None]
        cols = c0 + jnp.arange(MXU_COLS)[None, :]
        on_diag = (rows // bw) == (cols // bw)

        def window(w):
            rep = jnp.tile(w.reshape(width, bw)[ws:ws + BD_WINDOW], (1, reps))[:, c0 % bw:c0 % bw + MXU_COLS]
            return jnp.where(on_diag, rep, 0.0)

        tiles.append(jnp.concatenate([window(w_r), window(w_i)], axis=1))
        starts.append(ws)
    return jnp.stack(tiles).astype(BF16), tuple(starts)


def kernel(x, mem, g_mix, g_mem, w_mem_kv, g_q_mem, g_k_mem, w_in_a, w_conv_a, b_conv_a, w_r_a, b_r_a,
           w_i_a, b_i_a, lam_a, w_out_a, g_kv, w_kv, g_k_att, w_in_b, g_q_att, rel_bias_b, w_out_b,
           g_ffn, w_up, w_dw_ffn, b_dw_ffn, w_down):
    bsz, s, d = x.shape
    depth = g_mix.shape[0]
    n_a = w_in_a.shape[0]
    mem_hd = g_q_mem.shape[1]
    mem_w = w_mem_kv.shape[2] // 2
    mem_heads = mem_w // mem_hd
    att_hd = g_k_att.shape[0]
    att_w = w_kv.shape[1] // 2
    att_heads = att_w // att_hd
    d_rnn = w_conv_a.shape[2]
    assert att_hd * 2 == LANES and att_hd == CHUNK and bsz == SUBLANES
    assert s % TS_KV == 0 and s % TS_MIX == 0 and s % TQ == 0 and s % TT_FFN == 0 and s % TT_RGLRU == 0

    row = lambda v: v.reshape(1, -1).astype(F32)
    km_all, vm_all = _memkv_call(mem, g_mem, w_mem_kv, g_k_mem, mem_heads, mem_hd)

    g_ffn3 = g_ffn[:, None, :].astype(F32)
    b_dw3 = b_dw_ffn[:, None, :].astype(F32)
    w_up_bf = w_up.astype(BF16)
    w_down_bf = w_down.astype(BF16)
    k = vt = None
    for l in range(depth):
        if l < n_a:
            wbd, bd_starts = _block_diag_windows(w_r_a[l], w_i_a[l])
            w_in = w_in_a[l].astype(BF16)
            main = _rglru_call(x, row(g_mix[l]), w_in[:, :2 * d_rnn], w_conv_a[l], row(b_conv_a[l]), wbd,
                               bd_starts, row(b_r_a[l]), row(b_i_a[l]), row(lam_a[l]))
            x = _mixout_call(x, main, row(g_mix[l]), w_in[:, 2 * d_rnn:], km_all[l], vm_all[l],
                             row(g_q_mem[l]), w_out_a[l].astype(BF16), mem_heads, mem_hd)
        else:
            j = l - n_a
            if j == 0:
                k, vt = _kv_call(x, row(g_kv), w_kv[:, :att_w].astype(BF16), w_kv[:, att_w:].T.astype(BF16),
                                 row(jnp.tile(g_k_att, att_heads)), att_hd)
            gqa = jnp.broadcast_to((jnp.tile(g_q_att[j], att_heads) * (att_hd ** -0.5 * LOG2E))[:, None],
                                   (att_w, TQ)).astype(F32)
            x = _mixer_b_call(x, row(g_mix[l]), w_in_b[j].astype(BF16), gqa, k, vt, _bias_call(rel_bias_b[j]),
                              km_all[l], vm_all[l], row(g_q_mem[l]), w_out_b[j].astype(BF16),
                              att_hd, mem_heads, mem_hd)
        x = _ffn_call(x, l, g_ffn3, w_up_bf, w_dw_ffn, b_dw3, w_down_bf)
    return x
```

```python
import functools
import math

import jax
import jax.numpy as jnp
from jax import lax
from jax.experimental import pallas as pl
from jax.experimental.pallas import tpu as pltpu

EPS = 1e-6
CHUNK = 64
N_PREV_CHUNKS = 8
BAND = (N_PREV_CHUNKS + 1) * CHUNK
MAX_REL = 256
RG_C = 8.0
NEG = -1e30
LOG2E = math.log2(math.e)

LANES = 128
SUBLANES = 8
BF16_ROWS = 16
PITCH_PAD = 4
MXU_COLS = 256
BD_WINDOW = 512
VMEM_LIMIT = 56 * 1024 * 1024

TT_RGLRU = 32
TT_FFN = 64
FFN_SUB = 256
TS_MIX = 1024
TS_KV = 1024
TQ = 4 * CHUNK
KEY_ROWS = BAND + CHUNK
NEG_ROWS = TQ

F32 = jnp.float32
BF16 = jnp.bfloat16


def _cparams(sem):
    return pltpu.CompilerParams(dimension_semantics=sem, vmem_limit_bytes=VMEM_LIMIT)


def _dot(a, b):
    return jnp.dot(a, b, preferred_element_type=F32)


def _dot_nt(a, b):
    return lax.dot_general(a, b, (((1,), (1,)), ((), ())), preferred_element_type=F32)


def _rms(x, g):
    ms = jnp.mean(x * x, axis=-1, keepdims=True)
    return x * lax.rsqrt(ms + EPS) * g


def _gelu(x):
    c = -2.0 * 0.7978845608028654 * LOG2E
    return x / (1.0 + jnp.exp2(x * (c + (c * 0.044715) * (x * x))))


def _gelu_tanh(x):
    c = 0.7978845608028654
    return x * (0.5 + 0.5 * jnp.tanh(x * (c + (c * 0.044715) * (x * x))))


def _sigmoid(x):
    return 0.5 + 0.5 * jnp.tanh(0.5 * x)


def _to_time_major(x_ref, pad_ref):
    nb, tt, d = x_ref.shape
    pitch = pad_ref.shape[1] // nb
    for b in range(nb):
        for c in range(d // LANES):
            pad_ref[c, b * pitch:b * pitch + tt, :] = x_ref[b, :, c * LANES:(c + 1) * LANES].astype(F32)
    rows = []
    for t in range(tt):
        rows.append(jnp.concatenate(
            [pad_ref[c, pl.ds(t, nb, stride=pitch), :] for c in range(d // LANES)], axis=1))
    return jnp.concatenate(rows, axis=0)


def _from_time_major(y, o_ref, pad_ref, res_ref=None):
    nb, tt, d = o_ref.shape
    pitch = pad_ref.shape[1] // nb
    for t in range(tt):
        for c in range(d // LANES):
            pad_ref[c, pl.ds(t, nb, stride=pitch), :] = y[t * nb:(t + 1) * nb, c * LANES:(c + 1) * LANES]
    for b in range(nb):
        for c in range(d // LANES):
            v = pad_ref[c, b * pitch:b * pitch + tt, :]
            if res_ref is not None:
                v = v + res_ref[b, :, c * LANES:(c + 1) * LANES]
            o_ref[b, :, c * LANES:(c + 1) * LANES] = v.astype(o_ref.dtype)


def _causal_conv_tm(x, prev, w_ref, b_ref, nb, cols=slice(None)):
    n = x.shape[0]
    width = w_ref.shape[0]
    xc = jnp.concatenate([prev, x], axis=0)
    out = x * w_ref[width - 1:width, cols] + b_ref[:, cols]
    for k in range(width - 1):
        out = out + xc[k * nb:k * nb + n] * w_ref[k:k + 1, cols]
    return out


def _head_rms_rows(x, g, hd):
    t, w = x.shape
    lane = lax.broadcasted_iota(jnp.int32, (t, LANES), 1)
    lo = lane < hd
    outs = []
    for p in range(w // LANES):
        xp = x[:, p * LANES:(p + 1) * LANES]
        x2 = xp * xp
        s_lo = jnp.sum(jnp.where(lo, x2, 0.0), axis=-1, keepdims=True)
        s_hi = jnp.sum(jnp.where(lo, 0.0, x2), axis=-1, keepdims=True)
        r_lo = lax.rsqrt(s_lo * (1.0 / hd) + EPS)
        r_hi = lax.rsqrt(s_hi * (1.0 / hd) + EPS)
        outs.append(xp * jnp.where(lo, r_lo, r_hi))
    return jnp.concatenate(outs, axis=-1) * g


def _mem_attention(qm, km_ref, vm_ref, gq, n_heads, hd):
    outs = []
    for hh in range(n_heads):
        sl = slice(hh * hd, (hh + 1) * hd)
        qn = (_rms(qm[:, sl], gq) * (hd ** -0.5)).astype(BF16)
        s = _dot_nt(qn, km_ref[:, sl])
        m = jnp.max(s, axis=-1, keepdims=True)
        e = jnp.exp(s - m)
        l = jnp.sum(e, axis=-1, keepdims=True)
        o = _dot(e.astype(BF16), vm_ref[:, sl])
        outs.append(o / l)
    return jnp.concatenate(outs, axis=-1)


def _memkv_kernel(mem_ref, g_ref, w_ref, gk_ref, km_ref, vm_ref, *, n_heads, hd):
    bsz, n_mem, d = mem_ref.shape
    mn = _rms(mem_ref[...].reshape(bsz * n_mem, d), g_ref[...]).astype(BF16)
    kv = _dot(mn, w_ref[...])
    w = n_heads * hd
    km = jnp.concatenate([_rms(kv[:, hh * hd:(hh + 1) * hd], gk_ref[...]) for hh in range(n_heads)], axis=-1)
    km_ref[...] = km.astype(BF16).reshape(bsz, n_mem, w)
    vm_ref[...] = kv[:, w:].astype(BF16).reshape(bsz, n_mem, w)


def _memkv_call(mem, g_mem, w_mem_kv, g_k_mem, n_heads, hd):
    depth = g_mem.shape[0]
    bsz, n_mem, d = mem.shape
    w = n_heads * hd
    out = jax.ShapeDtypeStruct((depth, bsz, n_mem, w), BF16)
    return pl.pallas_call(
        functools.partial(_memkv_kernel, n_heads=n_heads, hd=hd),
        grid=(depth,),
        in_specs=[
            pl.BlockSpec((bsz, n_mem, d), lambda l: (0, 0, 0)),
            pl.BlockSpec((None, 1, d), lambda l: (l, 0, 0)),
            pl.BlockSpec((None, d, 2 * w), lambda l: (l, 0, 0)),
            pl.BlockSpec((None, 1, hd), lambda l: (l, 0, 0)),
        ],
        out_specs=[
            pl.BlockSpec((None, bsz, n_mem, w), lambda l: (l, 0, 0, 0)),
            pl.BlockSpec((None, bsz, n_mem, w), lambda l: (l, 0, 0, 0)),
        ],
        out_shape=[out, out],
        compiler_params=_cparams(("arbitrary",)),
        name="mem_kv",
    )(mem, g_mem.reshape(depth, 1, d), w_mem_kv.astype(BF16), g_k_mem.reshape(depth, 1, hd))


def _rglru_kernel(x_ref, g_ref, win_ref, wc_ref, bc_ref, wbd_ref, br_ref, bi_ref, lam_ref, o_ref,
                  pad_ref, xprev_ref, hprev_ref, *, bd_starts):
    @pl.when(pl.program_id(0) == 0)
    def _():
        xprev_ref[...] = jnp.zeros_like(xprev_ref)
        hprev_ref[...] = jnp.zeros_like(hprev_ref)

    nb, tt, _ = x_ref.shape
    d_rnn = wc_ref.shape[1]
    x = _to_time_major(x_ref, pad_ref)
    n = x.shape[0]
    h = _rms(x, g_ref[...]).astype(BF16)
    proj = _dot(h, win_ref[...])
    xr = proj[:, :d_rnn]
    yg = proj[:, d_rnn:]

    conv = _causal_conv_tm(xr, xprev_ref[...], wc_ref, bc_ref, nb)
    xprev_ref[...] = xr[n - xprev_ref.shape[0]:, :]

    cb = conv.astype(BF16)
    win = wbd_ref.shape[1]
    ris = [_dot(cb[:, ws:ws + win], wbd_ref[j]) for j, ws in enumerate(bd_starts)]
    r = _sigmoid(jnp.concatenate([ri[:, :MXU_COLS] for ri in ris], axis=-1) + br_ref[...])
    ig = _sigmoid(jnp.concatenate([ri[:, MXU_COLS:] for ri in ris], axis=-1) + bi_ref[...])
    lam = lam_ref[...]
    neg_c = -RG_C * (jnp.minimum(lam, 0.0) - jnp.log1p(jnp.exp(-jnp.abs(lam))))
    neg_log_a = r * neg_c
    a = jnp.exp2(r * (neg_c * -LOG2E))
    z = jnp.tanh(neg_log_a) * (a * a + 1.0)
    u = jnp.where(z > 0.0, z * lax.rsqrt(z), 0.0) * ig * conv

    hcur = hprev_ref[...]
    hs = []
    for t in range(tt):
        hcur = a[t * nb:(t + 1) * nb] * hcur + u[t * nb:(t + 1) * nb]
        hs.append(hcur)
    hprev_ref[...] = hcur
    main = jnp.concatenate(hs, axis=0) * _gelu_tanh(yg)
    _from_time_major(main, o_ref, pad_ref)


def _rglru_call(x, g, w_in_xy, w_conv, b_conv, wbd, bd_starts, b_r, b_i, lam):
    bsz, s, d = x.shape
    d_rnn = w_conv.shape[1]
    tt = TT_RGLRU
    const = lambda *shape: pl.BlockSpec(shape, lambda i: (0,) * len(shape))
    return pl.pallas_call(
        functools.partial(_rglru_kernel, bd_starts=bd_starts),
        grid=(s // tt,),
        in_specs=[
            pl.BlockSpec((bsz, tt, d), lambda i: (0, i, 0)),
            const(1, d),
            const(*w_in_xy.shape),
            const(*w_conv.shape),
            const(1, d_rnn),
            const(*wbd.shape),
            const(1, d_rnn),
            const(1, d_rnn),
            const(1, d_rnn),
        ],
        out_specs=pl.BlockSpec((bsz, tt, d_rnn), lambda i: (0, i, 0)),
        out_shape=jax.ShapeDtypeStruct((bsz, s, d_rnn), BF16),
        scratch_shapes=[
            pltpu.VMEM((d_rnn // LANES, bsz * (tt + PITCH_PAD), LANES), F32),
            pltpu.VMEM(((w_conv.shape[0] - 1) * bsz, d_rnn), F32),
            pltpu.VMEM((bsz, d_rnn), F32),
        ],
        compiler_params=_cparams(("arbitrary",)),
        name="rglru_a",
    )(x, g, w_in_xy, w_conv, b_conv, wbd, b_r, b_i, lam)


def _mixout_kernel(x_ref, main_ref, g_ref, wq_ref, km_ref, vm_ref, gq_ref, wout_ref, o_ref, *, n_heads, hd):
    x = x_ref[...]
    d_main = main_ref.shape[1]
    h = _rms(x, g_ref[...]).astype(BF16)
    qm = _dot(h, wq_ref[...])
    mo = _mem_attention(qm, km_ref, vm_ref, gq_ref[...], n_heads, hd).astype(BF16)
    y = _dot(main_ref[...], wout_ref[:d_main, :]) + _dot(mo, wout_ref[d_main:, :])
    o_ref[...] = x + y


def _mixout_call(x, main, g, w_q, km, vm, gq, w_out, n_heads, hd):
    bsz, s, d = x.shape
    d_main = main.shape[2]
    n_mem, mw = km.shape[1:]
    ts = TS_MIX
    const = lambda *shape: pl.BlockSpec(shape, lambda b, i: (0,) * len(shape))
    return pl.pallas_call(
        functools.partial(_mixout_kernel, n_heads=n_heads, hd=hd),
        grid=(bsz, s // ts),
        in_specs=[
            pl.BlockSpec((None, ts, d), lambda b, i: (b, i, 0)),
            pl.BlockSpec((None, ts, d_main), lambda b, i: (b, i, 0)),
            const(1, d),
            const(*w_q.shape),
            pl.BlockSpec((None, n_mem, mw), lambda b, i: (b, 0, 0)),
            pl.BlockSpec((None, n_mem, mw), lambda b, i: (b, 0, 0)),
            const(1, hd),
            const(*w_out.shape),
        ],
        out_specs=pl.BlockSpec((None, ts, d), lambda b, i: (b, i, 0)),
        out_shape=jax.ShapeDtypeStruct(x.shape, F32),
        compiler_params=_cparams(("arbitrary", "arbitrary")),
        name="mixout_a",
    )(x, main, g, w_q, km, vm, gq, w_out)


def _ffn_kernel(x_ref, g_ref, wu_ref, wg_ref, cu_ref, cg_ref, bu_ref, bg_ref, wd_ref, o_ref,
                pad_ref, carry_ref):
    nb = x_ref.shape[0]

    @pl.when(pl.program_id(0) == 0)
    def _():
        carry_ref[...] = jnp.zeros_like(carry_ref)

    hb = _rms(_to_time_major(x_ref, pad_ref), g_ref[...]).astype(BF16)
    n = hb.shape[0]
    keep = carry_ref.shape[1]

    def branch(w_ref, c_ref, b_ref, slot, cols):
        pre = _dot(hb, w_ref[:, cols])
        out = _causal_conv_tm(pre, carry_ref[slot, :, cols], c_ref, b_ref, nb, cols)
        carry_ref[slot, :, cols] = pre[n - keep:, :]
        return out

    acts = []
    for j in range(wu_ref.shape[1] // FFN_SUB):
        cols = slice(j * FFN_SUB, (j + 1) * FFN_SUB)
        u = branch(wu_ref, cu_ref, bu_ref, 0, cols)
        gt = branch(wg_ref, cg_ref, bg_ref, 1, cols)
        acts.append((_gelu(gt) * u).astype(BF16))
    act = jnp.concatenate(acts, axis=1)
    y = jnp.concatenate([_dot(act, wd_ref[:, j * FFN_SUB:(j + 1) * FFN_SUB])
                         for j in range(wd_ref.shape[1] // FFN_SUB)], axis=1)
    _from_time_major(y, o_ref, pad_ref, res_ref=x_ref)


def _ffn_call(x, layer, g, w_up, w_dw, b_dw, w_down):
    bsz, s, d = x.shape
    d_ff = w_down.shape[1]
    tt = TT_FFN
    width = w_dw.shape[1]
    resident = dict(pipeline_mode=pl.Buffered(1))
    return pl.pallas_call(
        _ffn_kernel,
        grid=(s // tt,),
        in_specs=[
            pl.BlockSpec((bsz, tt, d), lambda i: (0, i, 0)),
            pl.BlockSpec((None, 1, d), lambda i: (layer, 0, 0)),
            pl.BlockSpec((None, d, d_ff), lambda i: (layer, 0, 0), **resident),
            pl.BlockSpec((None, d, d_ff), lambda i: (layer, 0, 1), **resident),
            pl.BlockSpec((None, width, d_ff), lambda i: (layer, 0, 0)),
            pl.BlockSpec((None, width, d_ff), lambda i: (layer, 0, 1)),
            pl.BlockSpec((None, 1, d_ff), lambda i: (layer, 0, 0)),
            pl.BlockSpec((None, 1, d_ff), lambda i: (layer, 0, 1)),
            pl.BlockSpec((None, d_ff, d), lambda i: (layer, 0, 0), **resident),
        ],
        out_specs=pl.BlockSpec((bsz, tt, d), lambda i: (0, i, 0)),
        out_shape=jax.ShapeDtypeStruct(x.shape, F32),
        scratch_shapes=[
            pltpu.VMEM((d // LANES, bsz * (tt + PITCH_PAD), LANES), F32),
            pltpu.VMEM((2, (width - 1) * bsz, d_ff), F32),
        ],
        compiler_params=_cparams(("arbitrary",)),
        name="conv_ffn",
    )(x, g, w_up, w_up, w_dw, w_dw, b_dw, b_dw, w_down)


def _kv_kernel(x_ref, g_ref, wk_ref, wvt_ref, gk_ref, k_ref, vt_ref, *, hd):
    h = _rms(x_ref[...], g_ref[...]).astype(BF16)
    k = _dot(h, wk_ref[...])
    k_ref[...] = _head_rms_rows(k, gk_ref[...], hd).astype(BF16)
    vt = _dot_nt(wvt_ref[...], h)
    ext = hd + BF16_ROWS
    n_blk, _, tq = vt_ref.shape
    ones = jnp.ones((BF16_ROWS, tq), BF16)
    for j in range(n_blk):
        for hh in range(vt.shape[0] // hd):
            vt_ref[j, hh * ext:hh * ext + hd, :] = vt[hh * hd:(hh + 1) * hd, j * tq:(j + 1) * tq].astype(BF16)
            vt_ref[j, hh * ext + hd:(hh + 1) * ext, :] = ones


def _kv_call(x, g, w_k, w_vt, gk_row, hd):
    bsz, s, d = x.shape
    w = w_k.shape[1]
    wx = (w // hd) * (hd + BF16_ROWS)
    ts = TS_KV
    return pl.pallas_call(
        functools.partial(_kv_kernel, hd=hd),
        grid=(bsz, s // ts),
        in_specs=[
            pl.BlockSpec((None, ts, d), lambda b, i: (b, i, 0)),
            pl.BlockSpec((1, d), lambda b, i: (0, 0)),
            pl.BlockSpec((d, w), lambda b, i: (0, 0)),
            pl.BlockSpec((w, d), lambda b, i: (0, 0)),
            pl.BlockSpec((1, w), lambda b, i: (0, 0)),
        ],
        out_specs=[
            pl.BlockSpec((None, ts, w), lambda b, i: (b, i, 0)),
            pl.BlockSpec((None, ts // TQ, wx, TQ), lambda b, i: (b, i, 0, 0)),
        ],
        out_shape=[jax.ShapeDtypeStruct((bsz, s, w), BF16),
                   jax.ShapeDtypeStruct((bsz, s // TQ, wx, TQ), BF16)],
        compiler_params=_cparams(("arbitrary", "arbitrary")),
        name="kv_proj",
    )(x, g, w_k, w_vt, gk_row)


def _bias_kernel(f_ref, o_ref):
    x = jnp.broadcast_to(f_ref[0], (CHUNK, KEY_ROWS))
    r = pltpu.roll(x, 0, 1, stride=1, stride_axis=0)
    band = r.T[CHUNK - 1:CHUNK - 1 + BAND, :]
    negs = jnp.full((CHUNK, CHUNK), NEG, F32)
    lo = jnp.concatenate([band, negs], axis=0)
    hi = jnp.concatenate([negs, band], axis=0)
    o_ref[0, 0:NEG_ROWS, :] = jnp.full((NEG_ROWS, 2 * CHUNK), NEG, F32)
    o_ref[0, NEG_ROWS:, :] = jnp.concatenate([lo, hi], axis=1)


def _bias_call(rel_bias):
    n_heads, n_rel = rel_bias.shape
    assert n_rel == MAX_REL + CHUNK and 2 * CHUNK == LANES
    f = jnp.concatenate([jnp.broadcast_to(rel_bias[:, n_rel - 1:], (n_heads, KEY_ROWS - 1 - n_rel)),
                         rel_bias[:, ::-1], rel_bias[:, :1]], axis=1) * LOG2E
    rows = NEG_ROWS + KEY_ROWS
    return pl.pallas_call(
        _bias_kernel,
        grid=(n_heads,),
        in_specs=[pl.BlockSpec((1, 1, KEY_ROWS), lambda h: (h, 0, 0))],
        out_specs=pl.BlockSpec((1, rows, LANES), lambda h: (h, 0, 0)),
        out_shape=jax.ShapeDtypeStruct((n_heads, rows, LANES), F32),
        compiler_params=_cparams(("arbitrary",)),
        name="band_bias",
    )(f.astype(F32)[:, None, :])


def _mixer_b_kernel(x_ref, g_ref, win_ref, gqa_ref, k0_ref, k1_ref, k2_ref, v0_ref, v1_ref, v2_ref,
                    pb_ref, km_ref, vm_ref, gq_ref, wout_ref, o_ref, qa_ref, s_ref, p_ref, att_ref,
                    *, att_w, att_hd, n_heads, hd):
    i = pl.program_id(1)
    x = x_ref[...]
    tq = x.shape[0]
    half = tq // 2
    assert half == LANES
    h = _rms(x, g_ref[...]).astype(BF16)
    proj = _dot(h, win_ref[...])
    q_t = proj[:, :att_w].T
    qn = []
    for hh in range(att_w // att_hd):
        q_h = q_t[hh * att_hd:(hh + 1) * att_hd]
        ms = jnp.mean(q_h * q_h, axis=0, keepdims=True)
        qn.append(q_h * lax.rsqrt(ms + EPS))
    qa_ref[...] = pltpu.bitcast((jnp.concatenate(qn, axis=0) * gqa_ref[...]).astype(BF16), jnp.uint32)
    qm = proj[:, att_w:]

    k_refs = (k0_ref, k1_ref, k2_ref)
    v_refs = (v0_ref, v1_ref, v2_ref)
    ok0 = i >= 2
    ok1 = i >= 1
    ext = att_hd + BF16_ROWS

    zblk = jnp.zeros((half, half), BF16)
    for slot in range(2):
        p_ref[slot, 0, 0:half, half:] = zblk
        p_ref[slot, 2, half:, 0:half] = zblk

    def bias_rows(hh, ok, start, size):
        first = pl.multiple_of(jnp.where(ok, NEG_ROWS + start, 0), LANES)
        return pb_ref[hh, pl.ds(first, size), :]

    def scores(hh):
        slot = hh % 2
        col = (hh // 2) * LANES
        rows = att_hd // 2
        q_t = pltpu.bitcast(qa_ref[hh * rows:(hh + 1) * rows, :], BF16)
        qz = jnp.concatenate([q_t, jnp.zeros_like(q_t)] if hh % 2 == 0 else [jnp.zeros_like(q_t), q_t], axis=0)
        st = [_dot(kr[:, col:col + LANES], qz) for kr in k_refs]
        s_ref[slot, 0, 0:tq, :] = st[0][:, :half] + bias_rows(hh, ok0, 0, tq)
        s_ref[slot, 0, tq:2 * tq, :] = st[1][:, :half] + bias_rows(hh, ok1, tq, tq)
        s_ref[slot, 0, 2 * tq:, :] = st[2][:half, :half] + pb_ref[hh, NEG_ROWS + 2 * tq:, :]
        s_ref[slot, 1, 0:half, :] = st[0][half:, half:] + bias_rows(hh, ok0, 0, half)
        s_ref[slot, 1, half:half + tq, :] = st[1][:, half:] + bias_rows(hh, ok1, half, tq)
        s_ref[slot, 1, half + tq:, :] = st[2][:, half:] + pb_ref[hh, NEG_ROWS + half + tq:, :]

    def probs(hh):
        slot = hh % 2
        z = s_ref[slot, 0]
        e = jnp.exp2(z - jnp.max(z, axis=0, keepdims=True)).astype(BF16)
        p_ref[slot, 0, :, 0:half] = e[0:tq]
        p_ref[slot, 1, :, 0:half] = e[tq:2 * tq]
        p_ref[slot, 2, 0:half, 0:half] = e[2 * tq:]
        z = s_ref[slot, 1]
        e = jnp.exp2(z - jnp.max(z, axis=0, keepdims=True)).astype(BF16)
        p_ref[slot, 0, half:, half:] = e[0:half]
        p_ref[slot, 1, :, half:] = e[half:half + tq]
        p_ref[slot, 2, :, half:] = e[half + tq:]

    def values(hh):
        slot = hh % 2
        vt_h = jnp.concatenate([vr[hh * ext:(hh + 1) * ext, :] for vr in v_refs], axis=1)
        ot = _dot(vt_h, p_ref[slot].reshape(len(v_refs) * tq, tq))
        att_ref[hh * att_hd:(hh + 1) * att_hd, :] = ot[:att_hd] / ot[att_hd:att_hd + 1]

    n_att = att_w // att_hd
    for step in range(n_att + 2):
        if 0 <= step - 2:
            values(step - 2)
        if 0 <= step - 1 < n_att:
            probs(step - 1)
        if step < n_att:
            scores(step)

    att = att_ref[...].T.astype(BF16)
    mo = _mem_attention(qm, km_ref, vm_ref, gq_ref[...], n_heads, hd).astype(BF16)
    y = _dot(att, wout_ref[:att_w, :]) + _dot(mo, wout_ref[att_w:, :])
    o_ref[...] = x + y


def _mixer_b_call(x, g, w_in, gqa_row, k, vt, pb, km, vm, gq, w_out, att_hd, n_heads, hd):
    bsz, s, d = x.shape
    att_w = k.shape[2]
    vt_w = vt.shape[2]
    n_mem, mw = km.shape[1:]
    tq = TQ
    const = lambda *shape: pl.BlockSpec(shape, lambda b, i: (0,) * len(shape))
    kspec = lambda r: pl.BlockSpec((None, tq, att_w), lambda b, i: (b, jnp.maximum(i - 2 + r, 0), 0))
    vspec = lambda r: pl.BlockSpec((None, None, vt_w, tq), lambda b, i: (b, jnp.maximum(i - 2 + r, 0), 0, 0))
    return pl.pallas_call(
        functools.partial(_mixer_b_kernel, att_w=att_w, att_hd=att_hd, n_heads=n_heads, hd=hd),
        grid=(bsz, s // tq),
        in_specs=[
            pl.BlockSpec((None, tq, d), lambda b, i: (b, i, 0)),
            const(1, d),
            const(*w_in.shape),
            const(att_w, tq),
            kspec(0), kspec(1), kspec(2),
            vspec(0), vspec(1), vspec(2),
            const(*pb.shape),
            pl.BlockSpec((None, n_mem, mw), lambda b, i: (b, 0, 0)),
            pl.BlockSpec((None, n_mem, mw), lambda b, i: (b, 0, 0)),
            const(1, hd),
            const(*w_out.shape),
        ],
        out_specs=pl.BlockSpec((None, tq, d), lambda b, i: (b, i, 0)),
        out_shape=jax.ShapeDtypeStruct(x.shape, F32),
        scratch_shapes=[
            pltpu.VMEM((att_w // 2, tq), jnp.uint32),
            pltpu.VMEM((2, 2, KEY_ROWS, LANES), F32),
            pltpu.VMEM((2, 3, tq, tq), BF16),
            pltpu.VMEM((att_w, tq), F32),
        ],
        compiler_params=_cparams(("arbitrary", "arbitrary")),
        name="mixer_b",
    )(x, g, w_in, gqa_row, k, k, k, vt, vt, vt, pb, km, vm, gq, w_out)


def _block_diag_windows(w_r, w_i):
    n, bw, _ = w_r.shape
    width = n * bw
    assert width % MXU_COLS == 0
    reps = -(-MXU_COLS // bw) + 1
    starts, tiles = [], []
    for j in range(width // MXU_COLS):
        c0 = j * MXU_COLS
        lo = (c0 // bw) * bw
        hi = -((-(c0 + MXU_COLS)) // bw) * bw
        ws = min((lo // LANES) * LANES, width - BD_WINDOW)
        assert ws <= lo and hi <= ws + BD_WINDOW
        rows = ws + jnp.arange(BD_WINDOW)[:, None]
        cols = c0 + jnp.arange(MXU_COLS)[None, :]
        on_diag = (rows // bw) == (cols // bw)

        def window(w):
            rep = jnp.tile(w.reshape(width, bw)[ws:ws + BD_WINDOW], (1, reps))[:, c0 % bw:c0 % bw + MXU_COLS]
            return jnp.where(on_diag, rep, 0.0)

        tiles.append(jnp.concatenate([window(w_r), window(w_i)], axis=1))
        starts.append(ws)
    return jnp.stack(tiles).astype(BF16), tuple(starts)


def kernel(x, mem, g_mix, g_mem, w_mem_kv, g_q_mem, g_k_mem, w_in_a, w_conv_a, b_conv_a, w_r_a, b_r_a,
           w_i_a, b_i_a, lam_a, w_out_a, g_kv, w_kv, g_k_att, w_in_b, g_q_att, rel_bias_b, w_out_b,
           g_ffn, w_up, w_dw_ffn, b_dw_ffn, w_down):
    bsz, s, d = x.shape
    depth = g_mix.shape[0]
    n_a = w_in_a.shape[0]
    mem_hd = g_q_mem.shape[1]
    mem_w = w_mem_kv.shape[2] // 2
    mem_heads = mem_w // mem_hd
    att_hd = g_k_att.shape[0]
    att_w = w_kv.shape[1] // 2
    att_heads = att_w // att_hd
    d_rnn = w_conv_a.shape[2]
    assert att_hd * 2 == LANES and att_hd == CHUNK and bsz == SUBLANES
    assert s % TS_KV == 0 and s % TS_MIX == 0 and s % TQ == 0 and s % TT_FFN == 0 and s % TT_RGLRU == 0

    row = lambda v: v.reshape(1, -1).astype(F32)
    km_all, vm_all = _memkv_call(mem, g_mem, w_mem_kv, g_k_mem, mem_heads, mem_hd)

    g_ffn3 = g_ffn[:, None, :].astype(F32)
    b_dw3 = b_dw_ffn[:, None, :].astype(F32)
    w_up_bf = w_up.astype(BF16)
    w_down_bf = w_down.astype(BF16)
    k = vt = None
    for l in range(depth):
        if l < n_a:
            wbd, bd_starts = _block_diag_windows(w_r_a[l], w_i_a[l])
            w_in = w_in_a[l].astype(BF16)
            main = _rglru_call(x, row(g_mix[l]), w_in[:, :2 * d_rnn], w_conv_a[l], row(b_conv_a[l]), wbd,
                               bd_starts, row(b_r_a[l]), row(b_i_a[l]), row(lam_a[l]))
            x = _mixout_call(x, main, row(g_mix[l]), w_in[:, 2 * d_rnn:], km_all[l], vm_all[l],
                             row(g_q_mem[l]), w_out_a[l].astype(BF16), mem_heads, mem_hd)
        else:
            j = l - n_a
            if j == 0:
                k, vt = _kv_call(x, row(g_kv), w_kv[:, :att_w].astype(BF16), w_kv[:, att_w:].T.astype(BF16),
                                 row(jnp.tile(g_k_att, att_heads)), att_hd)
            gqa = jnp.broadcast_to((jnp.tile(g_q_att[j], att_heads) * (att_hd ** -0.5 * LOG2E))[:, None],
                                   (att_w, TQ)).astype(F32)
            x = _mixer_b_call(x, row(g_mix[l]), w_in_b[j].astype(BF16), gqa, k, vt, _bias_call(rel_bias_b[j]),
                              km_all[l], vm_all[l], row(g_q_mem[l]), w_out_b[j].astype(BF16),
                              att_hd, mem_heads, mem_hd)
        x = _ffn_call(x, l, g_ffn3, w_up_bf, w_dw_ffn, b_dw3, w_down_bf)
    return x
```

```python
import functools
import math

import jax
import jax.numpy as jnp
from jax import lax
from jax.experimental import pallas as pl
from jax.experimental.pallas import tpu as pltpu

EPS = 1e-6
CHUNK = 64
N_PREV_CHUNKS = 8
BAND = (N_PREV_CHUNKS + 1) * CHUNK
MAX_REL = 256
RG_C = 8.0
NEG = -1e30
LOG2E = math.log2(math.e)

LANES = 128
SUBLANES = 8
BF16_ROWS = 16
PITCH_PAD = 4
MXU_COLS = 256
BD_WINDOW = 512
VMEM_LIMIT = 56 * 1024 * 1024

TT_RGLRU = 32
TT_FFN = 128
FFN_SUB = 256
TS_MIX = 1024
TS_KV = 1024
TQ = 4 * CHUNK
KEY_ROWS = BAND + CHUNK
NEG_ROWS = TQ

F32 = jnp.float32
BF16 = jnp.bfloat16


def _cparams(sem):
    return pltpu.CompilerParams(dimension_semantics=sem, vmem_limit_bytes=VMEM_LIMIT)


def _dot(a, b):
    return jnp.dot(a, b, preferred_element_type=F32)


def _dot_nt(a, b):
    return lax.dot_general(a, b, (((1,), (1,)), ((), ())), preferred_element_type=F32)


def _rms(x, g):
    ms = jnp.mean(x * x, axis=-1, keepdims=True)
    return x * lax.rsqrt(ms + EPS) * g


def _gelu(x):
    c = -2.0 * 0.7978845608028654 * LOG2E
    return x / (1.0 + jnp.exp2(x * (c + (c * 0.044715) * (x * x))))


def _gelu_tanh(x):
    c = 0.7978845608028654
    return x * (0.5 + 0.5 * jnp.tanh(x * (c + (c * 0.044715) * (x * x))))


def _sigmoid(x):
    return 0.5 + 0.5 * jnp.tanh(0.5 * x)


def _to_time_major(x_ref, pad_ref):
    nb, tt, d = x_ref.shape
    pitch = pad_ref.shape[1] // nb
    for b in range(nb):
        for c in range(d // LANES):
            pad_ref[c, b * pitch:b * pitch + tt, :] = x_ref[b, :, c * LANES:(c + 1) * LANES].astype(F32)
    rows = []
    for t in range(tt):
        rows.append(jnp.concatenate(
            [pad_ref[c, pl.ds(t, nb, stride=pitch), :] for c in range(d // LANES)], axis=1))
    return jnp.concatenate(rows, axis=0)


def _from_time_major(y, o_ref, pad_ref, res_ref=None):
    nb, tt, d = o_ref.shape
    pitch = pad_ref.shape[1] // nb
    for t in range(tt):
        for c in range(d // LANES):
            pad_ref[c, pl.ds(t, nb, stride=pitch), :] = y[t * nb:(t + 1) * nb, c * LANES:(c + 1) * LANES]
    for b in range(nb):
        for c in range(d // LANES):
            v = pad_ref[c, b * pitch:b * pitch + tt, :]
            if res_ref is not None:
                v = v + res_ref[b, :, c * LANES:(c + 1) * LANES]
            o_ref[b, :, c * LANES:(c + 1) * LANES] = v.astype(o_ref.dtype)


def _causal_conv_tm(x, prev, w_ref, b_ref, nb, cols=slice(None)):
    n = x.shape[0]
    width = w_ref.shape[0]
    xc = jnp.concatenate([prev, x], axis=0)
    out = x * w_ref[width - 1:width, cols] + b_ref[:, cols]
    for k in range(width - 1):
        out = out + xc[k * nb:k * nb + n] * w_ref[k:k + 1, cols]
    return out


def _head_rms_rows(x, g, hd):
    t, w = x.shape
    lane = lax.broadcasted_iota(jnp.int32, (t, LANES), 1)
    lo = lane < hd
    outs = []
    for p in range(w // LANES):
        xp = x[:, p * LANES:(p + 1) * LANES]
        x2 = xp * xp
        s_lo = jnp.sum(jnp.where(lo, x2, 0.0), axis=-1, keepdims=True)
        s_hi = jnp.sum(jnp.where(lo, 0.0, x2), axis=-1, keepdims=True)
        r_lo = lax.rsqrt(s_lo * (1.0 / hd) + EPS)
        r_hi = lax.rsqrt(s_hi * (1.0 / hd) + EPS)
        outs.append(xp * jnp.where(lo, r_lo, r_hi))
    return jnp.concatenate(outs, axis=-1) * g


def _mem_attention(qm, km_ref, vm_ref, gq, n_heads, hd):
    outs = []
    for hh in range(n_heads):
        sl = slice(hh * hd, (hh + 1) * hd)
        qn = (_rms(qm[:, sl], gq) * (hd ** -0.5)).astype(BF16)
        s = _dot_nt(qn, km_ref[:, sl])
        m = jnp.max(s, axis=-1, keepdims=True)
        e = jnp.exp(s - m)
        l = jnp.sum(e, axis=-1, keepdims=True)
        o = _dot(e.astype(BF16), vm_ref[:, sl])
        outs.append(o / l)
    return jnp.concatenate(outs, axis=-1)


def _memkv_kernel(mem_ref, g_ref, w_ref, gk_ref, km_ref, vm_ref, *, n_heads, hd):
    bsz, n_mem, d = mem_ref.shape
    mn = _rms(mem_ref[...].reshape(bsz * n_mem, d), g_ref[...]).astype(BF16)
    kv = _dot(mn, w_ref[...])
    w = n_heads * hd
    km = jnp.concatenate([_rms(kv[:, hh * hd:(hh + 1) * hd], gk_ref[...]) for hh in range(n_heads)], axis=-1)
    km_ref[...] = km.astype(BF16).reshape(bsz, n_mem, w)
    vm_ref[...] = kv[:, w:].astype(BF16).reshape(bsz, n_mem, w)


def _memkv_call(mem, g_mem, w_mem_kv, g_k_mem, n_heads, hd):
    depth = g_mem.shape[0]
    bsz, n_mem, d = mem.shape
    w = n_heads * hd
    out = jax.ShapeDtypeStruct((depth, bsz, n_mem, w), BF16)
    return pl.pallas_call(
        functools.partial(_memkv_kernel, n_heads=n_heads, hd=hd),
        grid=(depth,),
        in_specs=[
            pl.BlockSpec((bsz, n_mem, d), lambda l: (0, 0, 0)),
            pl.BlockSpec((None, 1, d), lambda l: (l, 0, 0)),
            pl.BlockSpec((None, d, 2 * w), lambda l: (l, 0, 0)),
            pl.BlockSpec((None, 1, hd), lambda l: (l, 0, 0)),
        ],
        out_specs=[
            pl.BlockSpec((None, bsz, n_mem, w), lambda l: (l, 0, 0, 0)),
            pl.BlockSpec((None, bsz, n_mem, w), lambda l: (l, 0, 0, 0)),
        ],
        out_shape=[out, out],
        compiler_params=_cparams(("arbitrary",)),
        name="mem_kv",
    )(mem, g_mem.reshape(depth, 1, d), w_mem_kv.astype(BF16), g_k_mem.reshape(depth, 1, hd))


def _rglru_kernel(x_ref, g_ref, win_ref, wc_ref, bc_ref, wbd_ref, br_ref, bi_ref, lam_ref, o_ref,
                  pad_ref, xprev_ref, hprev_ref, *, bd_starts):
    @pl.when(pl.program_id(0) == 0)
    def _():
        xprev_ref[...] = jnp.zeros_like(xprev_ref)
        hprev_ref[...] = jnp.zeros_like(hprev_ref)

    nb, tt, _ = x_ref.shape
    d_rnn = wc_ref.shape[1]
    x = _to_time_major(x_ref, pad_ref)
    n = x.shape[0]
    h = _rms(x, g_ref[...]).astype(BF16)
    proj = _dot(h, win_ref[...])
    xr = proj[:, :d_rnn]
    yg = proj[:, d_rnn:]

    conv = _causal_conv_tm(xr, xprev_ref[...], wc_ref, bc_ref, nb)
    xprev_ref[...] = xr[n - xprev_ref.shape[0]:, :]

    cb = conv.astype(BF16)
    win = wbd_ref.shape[1]
    ris = [_dot(cb[:, ws:ws + win], wbd_ref[j]) for j, ws in enumerate(bd_starts)]
    r = _sigmoid(jnp.concatenate([ri[:, :MXU_COLS] for ri in ris], axis=-1) + br_ref[...])
    ig = _sigmoid(jnp.concatenate([ri[:, MXU_COLS:] for ri in ris], axis=-1) + bi_ref[...])
    lam = lam_ref[...]
    neg_c = -RG_C * (jnp.minimum(lam, 0.0) - jnp.log1p(jnp.exp(-jnp.abs(lam))))
    neg_log_a = r * neg_c
    a = jnp.exp2(r * (neg_c * -LOG2E))
    z = jnp.tanh(neg_log_a) * (a * a + 1.0)
    u = jnp.where(z > 0.0, z * lax.rsqrt(z), 0.0) * ig * conv

    hcur = hprev_ref[...]
    hs = []
    for t in range(tt):
        hcur = a[t * nb:(t + 1) * nb] * hcur + u[t * nb:(t + 1) * nb]
        hs.append(hcur)
    hprev_ref[...] = hcur
    main = jnp.concatenate(hs, axis=0) * _gelu_tanh(yg)
    _from_time_major(main, o_ref, pad_ref)


def _rglru_call(x, g, w_in_xy, w_conv, b_conv, wbd, bd_starts, b_r, b_i, lam):
    bsz, s, d = x.shape
    d_rnn = w_conv.shape[1]
    tt = TT_RGLRU
    const = lambda *shape: pl.BlockSpec(shape, lambda i: (0,) * len(shape))
    return pl.pallas_call(
        functools.partial(_rglru_kernel, bd_starts=bd_starts),
        grid=(s // tt,),
        in_specs=[
            pl.BlockSpec((bsz, tt, d), lambda i: (0, i, 0)),
            const(1, d),
            const(*w_in_xy.shape),
            const(*w_conv.shape),
            const(1, d_rnn),
            const(*wbd.shape),
            const(1, d_rnn),
            const(1, d_rnn),
            const(1, d_rnn),
        ],
        out_specs=pl.BlockSpec((bsz, tt, d_rnn), lambda i: (0, i, 0)),
        out_shape=jax.ShapeDtypeStruct((bsz, s, d_rnn), BF16),
        scratch_shapes=[
            pltpu.VMEM((d_rnn // LANES, bsz * (tt + PITCH_PAD), LANES), F32),
            pltpu.VMEM(((w_conv.shape[0] - 1) * bsz, d_rnn), F32),
            pltpu.VMEM((bsz, d_rnn), F32),
        ],
        compiler_params=_cparams(("arbitrary",)),
        name="rglru_a",
    )(x, g, w_in_xy, w_conv, b_conv, wbd, b_r, b_i, lam)


def _mixout_kernel(x_ref, main_ref, g_ref, wq_ref, km_ref, vm_ref, gq_ref, wout_ref, o_ref, *, n_heads, hd):
    x = x_ref[...]
    d_main = main_ref.shape[1]
    h = _rms(x, g_ref[...]).astype(BF16)
    qm = _dot(h, wq_ref[...])
    mo = _mem_attention(qm, km_ref, vm_ref, gq_ref[...], n_heads, hd).astype(BF16)
    y = _dot(main_ref[...], wout_ref[:d_main, :]) + _dot(mo, wout_ref[d_main:, :])
    o_ref[...] = x + y


def _mixout_call(x, main, g, w_q, km, vm, gq, w_out, n_heads, hd):
    bsz, s, d = x.shape
    d_main = main.shape[2]
    n_mem, mw = km.shape[1:]
    ts = TS_MIX
    const = lambda *shape: pl.BlockSpec(shape, lambda b, i: (0,) * len(shape))
    return pl.pallas_call(
        functools.partial(_mixout_kernel, n_heads=n_heads, hd=hd),
        grid=(bsz, s // ts),
        in_specs=[
            pl.BlockSpec((None, ts, d), lambda b, i: (b, i, 0)),
            pl.BlockSpec((None, ts, d_main), lambda b, i: (b, i, 0)),
            const(1, d),
            const(*w_q.shape),
            pl.BlockSpec((None, n_mem, mw), lambda b, i: (b, 0, 0)),
            pl.BlockSpec((None, n_mem, mw), lambda b, i: (b, 0, 0)),
            const(1, hd),
            const(*w_out.shape),
        ],
        out_specs=pl.BlockSpec((None, ts, d), lambda b, i: (b, i, 0)),
        out_shape=jax.ShapeDtypeStruct(x.shape, F32),
        compiler_params=_cparams(("arbitrary", "arbitrary")),
        name="mixout_a",
    )(x, main, g, w_q, km, vm, gq, w_out)


def _ffn_kernel(x_ref, g_ref, wu_ref, wg_ref, cu_ref, cg_ref, bu_ref, bg_ref, wd_ref, o_ref,
                pad_ref, carry_ref):
    nb = x_ref.shape[0]

    @pl.when(pl.program_id(0) == 0)
    def _():
        carry_ref[...] = jnp.zeros_like(carry_ref)

    hb = _rms(_to_time_major(x_ref, pad_ref), g_ref[...]).astype(BF16)
    n = hb.shape[0]
    keep = carry_ref.shape[1]

    def branch(w_ref, c_ref, b_ref, slot, cols):
        pre = _dot(hb, w_ref[:, cols])
        out = _causal_conv_tm(pre, carry_ref[slot, :, cols], c_ref, b_ref, nb, cols)
        carry_ref[slot, :, cols] = pre[n - keep:, :]
        return out

    acts = []
    for j in range(wu_ref.shape[1] // FFN_SUB):
        cols = slice(j * FFN_SUB, (j + 1) * FFN_SUB)
        u = branch(wu_ref, cu_ref, bu_ref, 0, cols)
        gt = branch(wg_ref, cg_ref, bg_ref, 1, cols)
        acts.append((_gelu(gt) * u).astype(BF16))
    act = jnp.concatenate(acts, axis=1)
    y = jnp.concatenate([_dot(act, wd_ref[:, j * FFN_SUB:(j + 1) * FFN_SUB])
                         for j in range(wd_ref.shape[1] // FFN_SUB)], axis=1)
    _from_time_major(y, o_ref, pad_ref, res_ref=x_ref)


def _ffn_call(x, layer, g, w_up, w_dw, b_dw, w_down):
    bsz, s, d = x.shape
    d_ff = w_down.shape[1]
    tt = TT_FFN
    width = w_dw.shape[1]
    resident = dict(pipeline_mode=pl.Buffered(1))
    return pl.pallas_call(
        _ffn_kernel,
        grid=(s // tt,),
        in_specs=[
            pl.BlockSpec((bsz, tt, d), lambda i: (0, i, 0)),
            pl.BlockSpec((None, 1, d), lambda i: (layer, 0, 0)),
            pl.BlockSpec((None, d, d_ff), lambda i: (layer, 0, 0), **resident),
            pl.BlockSpec((None, d, d_ff), lambda i: (layer, 0, 1), **resident),
            pl.BlockSpec((None, width, d_ff), lambda i: (layer, 0, 0)),
            pl.BlockSpec((None, width, d_ff), lambda i: (layer, 0, 1)),
            pl.BlockSpec((None, 1, d_ff), lambda i: (layer, 0, 0)),
            pl.BlockSpec((None, 1, d_ff), lambda i: (layer, 0, 1)),
            pl.BlockSpec((None, d_ff, d), lambda i: (layer, 0, 0), **resident),
        ],
        out_specs=pl.BlockSpec((bsz, tt, d), lambda i: (0, i, 0)),
        out_shape=jax.ShapeDtypeStruct(x.shape, F32),
        scratch_shapes=[
            pltpu.VMEM((d // LANES, bsz * (tt + PITCH_PAD), LANES), F32),
            pltpu.VMEM((2, (width - 1) * bsz, d_ff), F32),
        ],
        compiler_params=_cparams(("arbitrary",)),
        name="conv_ffn",
    )(x, g, w_up, w_up, w_dw, w_dw, b_dw, b_dw, w_down)


def _kv_kernel(x_ref, g_ref, wk_ref, wvt_ref, gk_ref, k_ref, vt_ref, *, hd):
    h = _rms(x_ref[...], g_ref[...]).astype(BF16)
    k = _dot(h, wk_ref[...])
    k_ref[...] = _head_rms_rows(k, gk_ref[...], hd).astype(BF16)
    vt = _dot_nt(wvt_ref[...], h)
    ext = hd + BF16_ROWS
    n_blk, _, tq = vt_ref.shape
    ones = jnp.ones((BF16_ROWS, tq), BF16)
    for j in range(n_blk):
        for hh in range(vt.shape[0] // hd):
            vt_ref[j, hh * ext:hh * ext + hd, :] = vt[hh * hd:(hh + 1) * hd, j * tq:(j + 1) * tq].astype(BF16)
            vt_ref[j, hh * ext + hd:(hh + 1) * ext, :] = ones


def _kv_call(x, g, w_k, w_vt, gk_row, hd):
    bsz, s, d = x.shape
    w = w_k.shape[1]
    wx = (w // hd) * (hd + BF16_ROWS)
    ts = TS_KV
    return pl.pallas_call(
        functools.partial(_kv_kernel, hd=hd),
        grid=(bsz, s // ts),
        in_specs=[
            pl.BlockSpec((None, ts, d), lambda b, i: (b, i, 0)),
            pl.BlockSpec((1, d), lambda b, i: (0, 0)),
            pl.BlockSpec((d, w), lambda b, i: (0, 0)),
            pl.BlockSpec((w, d), lambda b, i: (0, 0)),
            pl.BlockSpec((1, w), lambda b, i: (0, 0)),
        ],
        out_specs=[
            pl.BlockSpec((None, ts, w), lambda b, i: (b, i, 0)),
            pl.BlockSpec((None, ts // TQ, wx, TQ), lambda b, i: (b, i, 0, 0)),
        ],
        out_shape=[jax.ShapeDtypeStruct((bsz, s, w), BF16),
                   jax.ShapeDtypeStruct((bsz, s // TQ, wx, TQ), BF16)],
        compiler_params=_cparams(("arbitrary", "arbitrary")),
        name="kv_proj",
    )(x, g, w_k, w_vt, gk_row)


def _bias_kernel(f_ref, o_ref):
    x = jnp.broadcast_to(f_ref[0], (CHUNK, KEY_ROWS))
    r = pltpu.roll(x, 0, 1, stride=1, stride_axis=0)
    band = r.T[CHUNK - 1:CHUNK - 1 + BAND, :]
    negs = jnp.full((CHUNK, CHUNK), NEG, F32)
    lo = jnp.concatenate([band, negs], axis=0)
    hi = jnp.concatenate([negs, band], axis=0)
    o_ref[0, 0:NEG_ROWS, :] = jnp.full((NEG_ROWS, 2 * CHUNK), NEG, F32)
    o_ref[0, NEG_ROWS:, :] = jnp.concatenate([lo, hi], axis=1)


def _bias_call(rel_bias):
    n_heads, n_rel = rel_bias.shape
    assert n_rel == MAX_REL + CHUNK and 2 * CHUNK == LANES
    f = jnp.concatenate([jnp.broadcast_to(rel_bias[:, n_rel - 1:], (n_heads, KEY_ROWS - 1 - n_rel)),
                         rel_bias[:, ::-1], rel_bias[:, :1]], axis=1) * LOG2E
    rows = NEG_ROWS + KEY_ROWS
    return pl.pallas_call(
        _bias_kernel,
        grid=(n_heads,),
        in_specs=[pl.BlockSpec((1, 1, KEY_ROWS), lambda h: (h, 0, 0))],
        out_specs=pl.BlockSpec((1, rows, LANES), lambda h: (h, 0, 0)),
        out_shape=jax.ShapeDtypeStruct((n_heads, rows, LANES), F32),
        compiler_params=_cparams(("arbitrary",)),
        name="band_bias",
    )(f.astype(F32)[:, None, :])


def _mixer_b_kernel(x_ref, g_ref, win_ref, gqa_ref, k0_ref, k1_ref, k2_ref, v0_ref, v1_ref, v2_ref,
                    pb_ref, km_ref, vm_ref, gq_ref, wout_ref, o_ref, qa_ref, s_ref, p_ref, att_ref,
                    *, att_w, att_hd, n_heads, hd):
    i = pl.program_id(1)
    x = x_ref[...]
    tq = x.shape[0]
    half = tq // 2
    assert half == LANES
    h = _rms(x, g_ref[...]).astype(BF16)
    proj = _dot(h, win_ref[...])
    q_t = proj[:, :att_w].T
    qn = []
    for hh in range(att_w // att_hd):
        q_h = q_t[hh * att_hd:(hh + 1) * att_hd]
        ms = jnp.mean(q_h * q_h, axis=0, keepdims=True)
        qn.append(q_h * lax.rsqrt(ms + EPS))
    qa_ref[...] = pltpu.bitcast((jnp.concatenate(qn, axis=0) * gqa_ref[...]).astype(BF16), jnp.uint32)
    qm = proj[:, att_w:]

    k_refs = (k0_ref, k1_ref, k2_ref)
    v_refs = (v0_ref, v1_ref, v2_ref)
    ok0 = i >= 2
    ok1 = i >= 1
    ext = att_hd + BF16_ROWS

    zblk = jnp.zeros((half, half), BF16)
    for slot in range(2):
        p_ref[slot, 0, 0:half, half:] = zblk
        p_ref[slot, 2, half:, 0:half] = zblk

    def bias_rows(hh, ok, start, size):
        first = pl.multiple_of(jnp.where(ok, NEG_ROWS + start, 0), LANES)
        return pb_ref[hh, pl.ds(first, size), :]

    def scores(hh):
        slot = hh % 2
        col = (hh // 2) * LANES
        rows = att_hd // 2
        q_t = pltpu.bitcast(qa_ref[hh * rows:(hh + 1) * rows, :], BF16)
        qz = jnp.concatenate([q_t, jnp.zeros_like(q_t)] if hh % 2 == 0 else [jnp.zeros_like(q_t), q_t], axis=0)
        st = [_dot(kr[:, col:col + LANES], qz) for kr in k_refs]
        s_ref[slot, 0, 0:tq, :] = st[0][:, :half] + bias_rows(hh, ok0, 0, tq)
        s_ref[slot, 0, tq:2 * tq, :] = st[1][:, :half] + bias_rows(hh, ok1, tq, tq)
        s_ref[slot, 0, 2 * tq:, :] = st[2][:half, :half] + pb_ref[hh, NEG_ROWS + 2 * tq:, :]
        s_ref[slot, 1, 0:half, :] = st[0][half:, half:] + bias_rows(hh, ok0, 0, half)
        s_ref[slot, 1, half:half + tq, :] = st[1][:, half:] + bias_rows(hh, ok1, half, tq)
        s_ref[slot, 1, half + tq:, :] = st[2][:, half:] + pb_ref[hh, NEG_ROWS + half + tq:, :]

    def probs(hh):
        slot = hh % 2
        z = s_ref[slot, 0]
        e = jnp.exp2(z - jnp.max(z, axis=0, keepdims=True)).astype(BF16)
        p_ref[slot, 0, :, 0:half] = e[0:tq]
        p_ref[slot, 1, :, 0:half] = e[tq:2 * tq]
        p_ref[slot, 2, 0:half, 0:half] = e[2 * tq:]
        z = s_ref[slot, 1]
        e = jnp.exp2(z - jnp.max(z, axis=0, keepdims=True)).astype(BF16)
        p_ref[slot, 0, half:, half:] = e[0:half]
        p_ref[slot, 1, :, half:] = e[half:half + tq]
        p_ref[slot, 2, :, half:] = e[half + tq:]

    def values(hh):
        slot = hh % 2
        vt_h = jnp.concatenate([vr[hh * ext:(hh + 1) * ext, :] for vr in v_refs], axis=1)
        ot = _dot(vt_h, p_ref[slot].reshape(len(v_refs) * tq, tq))
        att_ref[hh * att_hd:(hh + 1) * att_hd, :] = ot[:att_hd] / ot[att_hd:att_hd + 1]

    n_att = att_w // att_hd
    for step in range(n_att + 2):
        if 0 <= step - 2:
            values(step - 2)
        if 0 <= step - 1 < n_att:
            probs(step - 1)
        if step < n_att:
            scores(step)

    att = att_ref[...].T.astype(BF16)
    mo = _mem_attention(qm, km_ref, vm_ref, gq_ref[...], n_heads, hd).astype(BF16)
    y = _dot(att, wout_ref[:att_w, :]) + _dot(mo, wout_ref[att_w:, :])
    o_ref[...] = x + y


def _mixer_b_call(x, g, w_in, gqa_row, k, vt, pb, km, vm, gq, w_out, att_hd, n_heads, hd):
    bsz, s, d = x.shape
    att_w = k.shape[2]
    vt_w = vt.shape[2]
    n_mem, mw = km.shape[1:]
    tq = TQ
    const = lambda *shape: pl.BlockSpec(shape, lambda b, i: (0,) * len(shape))
    kspec = lambda r: pl.BlockSpec((None, tq, att_w), lambda b, i: (b, jnp.maximum(i - 2 + r, 0), 0))
    vspec = lambda r: pl.BlockSpec((None, None, vt_w, tq), lambda b, i: (b, jnp.maximum(i - 2 + r, 0), 0, 0))
    return pl.pallas_call(
        functools.partial(_mixer_b_kernel, att_w=att_w, att_hd=att_hd, n_heads=n_heads, hd=hd),
        grid=(bsz, s // tq),
        in_specs=[
            pl.BlockSpec((None, tq, d), lambda b, i: (b, i, 0)),
            const(1, d),
            const(*w_in.shape),
            const(att_w, tq),
            kspec(0), kspec(1), kspec(2),
            vspec(0), vspec(1), vspec(2),
            const(*pb.shape),
            pl.BlockSpec((None, n_mem, mw), lambda b, i: (b, 0, 0)),
            pl.BlockSpec((None, n_mem, mw), lambda b, i: (b, 0, 0)),
            const(1, hd),
            const(*w_out.shape),
        ],
        out_specs=pl.BlockSpec((None, tq, d), lambda b, i: (b, i, 0)),
        out_shape=jax.ShapeDtypeStruct(x.shape, F32),
        scratch_shapes=[
            pltpu.VMEM((att_w // 2, tq), jnp.uint32),
            pltpu.VMEM((2, 2, KEY_ROWS, LANES), F32),
            pltpu.VMEM((2, 3, tq, tq), BF16),
            pltpu.VMEM((att_w, tq), F32),
        ],
        compiler_params=_cparams(("arbitrary", "arbitrary")),
        name="mixer_b",
    )(x, g, w_in, gqa_row, k, k, k, vt, vt, vt, pb, km, vm, gq, w_out)


def _block_diag_windows(w_r, w_i):
    n, bw, _ = w_r.shape
    width = n * bw
    assert width % MXU_COLS == 0
    reps = -(-MXU_COLS // bw) + 1
    starts, tiles = [], []
    for j in range(width // MXU_COLS):
        c0 = j * MXU_COLS
        lo = (c0 // bw) * bw
        hi = -((-(c0 + MXU_COLS)) // bw) * bw
        ws = min((lo // LANES) * LANES, width - BD_WINDOW)
        assert ws <= lo and hi <= ws + BD_WINDOW
        rows = ws + jnp.arange(BD_WINDOW)[:, None]
        cols = c0 + jnp.arange(MXU_COLS)[None, :]
        on_diag = (rows // bw) == (cols // bw)

        def window(w):
            rep = jnp.tile(w.reshape(width, bw)[ws:ws + BD_WINDOW], (1, reps))[:, c0 % bw:c0 % bw + MXU_COLS]
            return jnp.where(on_diag, rep, 0.0)

        tiles.append(jnp.concatenate([window(w_r), window(w_i)], axis=1))
        starts.append(ws)
    return jnp.stack(tiles).astype(BF16), tuple(starts)


def kernel(x, mem, g_mix, g_mem, w_mem_kv, g_q_mem, g_k_mem, w_in_a, w_conv_a, b_conv_a, w_r_a, b_r_a,
           w_i_a, b_i_a, lam_a, w_out_a, g_kv, w_kv, g_k_att, w_in_b, g_q_att, rel_bias_b, w_out_b,
           g_ffn, w_up, w_dw_ffn, b_dw_ffn, w_down):
    bsz, s, d = x.shape
    depth = g_mix.shape[0]
    n_a = w_in_a.shape[0]
    mem_hd = g_q_mem.shape[1]
    mem_w = w_mem_kv.shape[2] // 2
    mem_heads = mem_w // mem_hd
    att_hd = g_k_att.shape[0]
    att_w = w_kv.shape[1] // 2
    att_heads = att_w // att_hd
    d_rnn = w_conv_a.shape[2]
    assert att_hd * 2 == LANES and att_hd == CHUNK and bsz == SUBLANES
    assert s % TS_KV == 0 and s % TS_MIX == 0 and s % TQ == 0 and s % TT_FFN == 0 and s % TT_RGLRU == 0

    row = lambda v: v.reshape(1, -1).astype(F32)
    km_all, vm_all = _memkv_call(mem, g_mem, w_mem_kv, g_k_mem, mem_heads, mem_hd)

    g_ffn3 = g_ffn[:, None, :].astype(F32)
    b_dw3 = b_dw_ffn[:, None, :].astype(F32)
    w_up_bf = w_up.astype(BF16)
    w_down_bf = w_down.astype(BF16)
    k = vt = None
    for l in range(depth):
        if l < n_a:
            wbd, bd_starts = _block_diag_windows(w_r_a[l], w_i_a[l])
            w_in = w_in_a[l].astype(BF16)
            main = _rglru_call(x, row(g_mix[l]), w_in[:, :2 * d_rnn], w_conv_a[l], row(b_conv_a[l]), wbd,
                               bd_starts, row(b_r_a[l]), row(b_i_a[l]), row(lam_a[l]))
            x = _mixout_call(x, main, row(g_mix[l]), w_in[:, 2 * d_rnn:], km_all[l], vm_all[l],
                             row(g_q_mem[l]), w_out_a[l].astype(BF16), mem_heads, mem_hd)
        else:
            j = l - n_a
            if j == 0:
                k, vt = _kv_call(x, row(g_kv), w_kv[:, :att_w].astype(BF16), w_kv[:, att_w:].T.astype(BF16),
                                 row(jnp.tile(g_k_att, att_heads)), att_hd)
            gqa = jnp.broadcast_to((jnp.tile(g_q_att[j], att_heads) * (att_hd ** -0.5 * LOG2E))[:, None],
                                   (att_w, TQ)).astype(F32)
            x = _mixer_b_call(x, row(g_mix[l]), w_in_b[j].astype(BF16), gqa, k, vt, _bias_call(rel_bias_b[j]),
                              km_all[l], vm_all[l], row(g_q_mem[l]), w_out_b[j].astype(BF16),
                              att_hd, mem_heads, mem_hd)
        x = _ffn_call(x, l, g_ffn3, w_up_bf, w_dw_ffn, b_dw3, w_down_bf)
    return x
```

```python
import functools
import math

import jax
import jax.numpy as jnp
from jax import lax
from jax.experimental import pallas as pl
from jax.experimental.pallas import tpu as pltpu

EPS = 1e-6
CHUNK = 64
N_PREV_CHUNKS = 8
BAND = (N_PREV_CHUNKS + 1) * CHUNK
MAX_REL = 256
RG_C = 8.0
NEG = -1e30
LOG2E = math.log2(math.e)

LANES = 128
SUBLANES = 8
BF16_ROWS = 16
PITCH_PAD = 4
MXU_COLS = 256
BD_WINDOW = 512
VMEM_LIMIT = 56 * 1024 * 1024

TT_RGLRU = 64
TT_FFN = 128
FFN_SUB = 256
TS_MIX = 1024
TS_KV = 1024
TQ = 4 * CHUNK
KEY_ROWS = BAND + CHUNK
NEG_ROWS = TQ

F32 = jnp.float32
BF16 = jnp.bfloat16


def _cparams(sem):
    return pltpu.CompilerParams(dimension_semantics=sem, vmem_limit_bytes=VMEM_LIMIT)


def _dot(a, b):
    return jnp.dot(a, b, preferred_element_type=F32)


def _dot_nt(a, b):
    return lax.dot_general(a, b, (((1,), (1,)), ((), ())), preferred_element_type=F32)


def _rms(x, g):
    ms = jnp.mean(x * x, axis=-1, keepdims=True)
    return x * lax.rsqrt(ms + EPS) * g


def _gelu(x):
    c = -2.0 * 0.7978845608028654 * LOG2E
    return x / (1.0 + jnp.exp2(x * (c + (c * 0.044715) * (x * x))))


def _gelu_tanh(x):
    c = 0.7978845608028654
    return x * (0.5 + 0.5 * jnp.tanh(x * (c + (c * 0.044715) * (x * x))))


def _sigmoid(x):
    return 0.5 + 0.5 * jnp.tanh(0.5 * x)


def _to_time_major(x_ref, pad_ref):
    nb, tt, d = x_ref.shape
    pitch = pad_ref.shape[1] // nb
    for b in range(nb):
        for c in range(d // LANES):
            pad_ref[c, b * pitch:b * pitch + tt, :] = x_ref[b, :, c * LANES:(c + 1) * LANES].astype(F32)
    rows = []
    for t in range(tt):
        rows.append(jnp.concatenate(
            [pad_ref[c, pl.ds(t, nb, stride=pitch), :] for c in range(d // LANES)], axis=1))
    return jnp.concatenate(rows, axis=0)


def _from_time_major(y, o_ref, pad_ref, res_ref=None):
    nb, tt, d = o_ref.shape
    pitch = pad_ref.shape[1] // nb
    for t in range(tt):
        for c in range(d // LANES):
            pad_ref[c, pl.ds(t, nb, stride=pitch), :] = y[t * nb:(t + 1) * nb, c * LANES:(c + 1) * LANES]
    for b in range(nb):
        for c in range(d // LANES):
            v = pad_ref[c, b * pitch:b * pitch + tt, :]
            if res_ref is not None:
                v = v + res_ref[b, :, c * LANES:(c + 1) * LANES]
            o_ref[b, :, c * LANES:(c + 1) * LANES] = v.astype(o_ref.dtype)


def _causal_conv_tm(x, prev, w_ref, b_ref, nb, cols=slice(None)):
    n = x.shape[0]
    width = w_ref.shape[0]
    xc = jnp.concatenate([prev, x], axis=0)
    out = x * w_ref[width - 1:width, cols] + b_ref[:, cols]
    for k in range(width - 1):
        out = out + xc[k * nb:k * nb + n] * w_ref[k:k + 1, cols]
    return out


def _head_rms_rows(x, g, hd):
    t, w = x.shape
    lane = lax.broadcasted_iota(jnp.int32, (t, LANES), 1)
    lo = lane < hd
    outs = []
    for p in range(w // LANES):
        xp = x[:, p * LANES:(p + 1) * LANES]
        x2 = xp * xp
        s_lo = jnp.sum(jnp.where(lo, x2, 0.0), axis=-1, keepdims=True)
        s_hi = jnp.sum(jnp.where(lo, 0.0, x2), axis=-1, keepdims=True)
        r_lo = lax.rsqrt(s_lo * (1.0 / hd) + EPS)
        r_hi = lax.rsqrt(s_hi * (1.0 / hd) + EPS)
        outs.append(xp * jnp.where(lo, r_lo, r_hi))
    return jnp.concatenate(outs, axis=-1) * g


def _mem_attention(qm, km_ref, vm_ref, gq, n_heads, hd):
    outs = []
    for hh in range(n_heads):
        sl = slice(hh * hd, (hh + 1) * hd)
        qn = (_rms(qm[:, sl], gq) * (hd ** -0.5)).astype(BF16)
        s = _dot_nt(qn, km_ref[:, sl])
        m = jnp.max(s, axis=-1, keepdims=True)
        e = jnp.exp(s - m)
        l = jnp.sum(e, axis=-1, keepdims=True)
        o = _dot(e.astype(BF16), vm_ref[:, sl])
        outs.append(o / l)
    return jnp.concatenate(outs, axis=-1)


def _memkv_kernel(mem_ref, g_ref, w_ref, gk_ref, km_ref, vm_ref, *, n_heads, hd):
    bsz, n_mem, d = mem_ref.shape
    mn = _rms(mem_ref[...].reshape(bsz * n_mem, d), g_ref[...]).astype(BF16)
    kv = _dot(mn, w_ref[...])
    w = n_heads * hd
    km = jnp.concatenate([_rms(kv[:, hh * hd:(hh + 1) * hd], gk_ref[...]) for hh in range(n_heads)], axis=-1)
    km_ref[...] = km.astype(BF16).reshape(bsz, n_mem, w)
    vm_ref[...] = kv[:, w:].astype(BF16).reshape(bsz, n_mem, w)


def _memkv_call(mem, g_mem, w_mem_kv, g_k_mem, n_heads, hd):
    depth = g_mem.shape[0]
    bsz, n_mem, d = mem.shape
    w = n_heads * hd
    out = jax.ShapeDtypeStruct((depth, bsz, n_mem, w), BF16)
    return pl.pallas_call(
        functools.partial(_memkv_kernel, n_heads=n_heads, hd=hd),
        grid=(depth,),
        in_specs=[
            pl.BlockSpec((bsz, n_mem, d), lambda l: (0, 0, 0)),
            pl.BlockSpec((None, 1, d), lambda l: (l, 0, 0)),
            pl.BlockSpec((None, d, 2 * w), lambda l: (l, 0, 0)),
            pl.BlockSpec((None, 1, hd), lambda l: (l, 0, 0)),
        ],
        out_specs=[
            pl.BlockSpec((None, bsz, n_mem, w), lambda l: (l, 0, 0, 0)),
            pl.BlockSpec((None, bsz, n_mem, w), lambda l: (l, 0, 0, 0)),
        ],
        out_shape=[out, out],
        compiler_params=_cparams(("arbitrary",)),
        name="mem_kv",
    )(mem, g_mem.reshape(depth, 1, d), w_mem_kv.astype(BF16), g_k_mem.reshape(depth, 1, hd))


def _rglru_kernel(x_ref, g_ref, win_ref, wc_ref, bc_ref, wbd_ref, br_ref, bi_ref, lam_ref, o_ref,
                  pad_ref, xprev_ref, hprev_ref, *, bd_starts):
    @pl.when(pl.program_id(0) == 0)
    def _():
        xprev_ref[...] = jnp.zeros_like(xprev_ref)
        hprev_ref[...] = jnp.zeros_like(hprev_ref)

    nb, tt, _ = x_ref.shape
    d_rnn = wc_ref.shape[1]
    x = _to_time_major(x_ref, pad_ref)
    n = x.shape[0]
    h = _rms(x, g_ref[...]).astype(BF16)
    proj = _dot(h, win_ref[...])
    xr = proj[:, :d_rnn]
    yg = proj[:, d_rnn:]

    conv = _causal_conv_tm(xr, xprev_ref[...], wc_ref, bc_ref, nb)
    xprev_ref[...] = xr[n - xprev_ref.shape[0]:, :]

    cb = conv.astype(BF16)
    win = wbd_ref.shape[1]
    ris = [_dot(cb[:, ws:ws + win], wbd_ref[j]) for j, ws in enumerate(bd_starts)]
    r = _sigmoid(jnp.concatenate([ri[:, :MXU_COLS] for ri in ris], axis=-1) + br_ref[...])
    ig = _sigmoid(jnp.concatenate([ri[:, MXU_COLS:] for ri in ris], axis=-1) + bi_ref[...])
    lam = lam_ref[...]
    neg_c = -RG_C * (jnp.minimum(lam, 0.0) - jnp.log1p(jnp.exp(-jnp.abs(lam))))
    neg_log_a = r * neg_c
    a = jnp.exp2(r * (neg_c * -LOG2E))
    z = jnp.tanh(neg_log_a) * (a * a + 1.0)
    u = jnp.where(z > 0.0, z * lax.rsqrt(z), 0.0) * ig * conv

    hcur = hprev_ref[...]
    hs = []
    for t in range(tt):
        hcur = a[t * nb:(t + 1) * nb] * hcur + u[t * nb:(t + 1) * nb]
        hs.append(hcur)
    hprev_ref[...] = hcur
    main = jnp.concatenate(hs, axis=0) * _gelu_tanh(yg)
    _from_time_major(main, o_ref, pad_ref)


def _rglru_call(x, g, w_in_xy, w_conv, b_conv, wbd, bd_starts, b_r, b_i, lam):
    bsz, s, d = x.shape
    d_rnn = w_conv.shape[1]
    tt = TT_RGLRU
    const = lambda *shape: pl.BlockSpec(shape, lambda i: (0,) * len(shape))
    return pl.pallas_call(
        functools.partial(_rglru_kernel, bd_starts=bd_starts),
        grid=(s // tt,),
        in_specs=[
            pl.BlockSpec((bsz, tt, d), lambda i: (0, i, 0)),
            const(1, d),
            pl.BlockSpec(w_in_xy.shape, lambda i: (0, 0), pipeline_mode=pl.Buffered(1)),
            const(*w_conv.shape),
            const(1, d_rnn),
            pl.BlockSpec(wbd.shape, lambda i: (0, 0, 0), pipeline_mode=pl.Buffered(1)),
            const(1, d_rnn),
            const(1, d_rnn),
            const(1, d_rnn),
        ],
        out_specs=pl.BlockSpec((bsz, tt, d_rnn), lambda i: (0, i, 0)),
        out_shape=jax.ShapeDtypeStruct((bsz, s, d_rnn), BF16),
        scratch_shapes=[
            pltpu.VMEM((d_rnn // LANES, bsz * (tt + PITCH_PAD), LANES), F32),
            pltpu.VMEM(((w_conv.shape[0] - 1) * bsz, d_rnn), F32),
            pltpu.VMEM((bsz, d_rnn), F32),
        ],
        compiler_params=_cparams(("arbitrary",)),
        name="rglru_a",
    )(x, g, w_in_xy, w_conv, b_conv, wbd, b_r, b_i, lam)


def _mixout_kernel(x_ref, main_ref, g_ref, wq_ref, km_ref, vm_ref, gq_ref, wout_ref, o_ref, *, n_heads, hd):
    x = x_ref[...]
    d_main = main_ref.shape[1]
    h = _rms(x, g_ref[...]).astype(BF16)
    qm = _dot(h, wq_ref[...])
    mo = _mem_attention(qm, km_ref, vm_ref, gq_ref[...], n_heads, hd).astype(BF16)
    y = _dot(main_ref[...], wout_ref[:d_main, :]) + _dot(mo, wout_ref[d_main:, :])
    o_ref[...] = x + y


def _mixout_call(x, main, g, w_q, km, vm, gq, w_out, n_heads, hd):
    bsz, s, d = x.shape
    d_main = main.shape[2]
    n_mem, mw = km.shape[1:]
    ts = TS_MIX
    const = lambda *shape: pl.BlockSpec(shape, lambda b, i: (0,) * len(shape))
    return pl.pallas_call(
        functools.partial(_mixout_kernel, n_heads=n_heads, hd=hd),
        grid=(bsz, s // ts),
        in_specs=[
            pl.BlockSpec((None, ts, d), lambda b, i: (b, i, 0)),
            pl.BlockSpec((None, ts, d_main), lambda b, i: (b, i, 0)),
            const(1, d),
            const(*w_q.shape),
            pl.BlockSpec((None, n_mem, mw), lambda b, i: (b, 0, 0)),
            pl.BlockSpec((None, n_mem, mw), lambda b, i: (b, 0, 0)),
            const(1, hd),
            const(*w_out.shape),
        ],
        out_specs=pl.BlockSpec((None, ts, d), lambda b, i: (b, i, 0)),
        out_shape=jax.ShapeDtypeStruct(x.shape, F32),
        compiler_params=_cparams(("arbitrary", "arbitrary")),
        name="mixout_a",
    )(x, main, g, w_q, km, vm, gq, w_out)


def _ffn_kernel(x_ref, g_ref, wu_ref, wg_ref, cu_ref, cg_ref, bu_ref, bg_ref, wd_ref, o_ref,
                pad_ref, carry_ref):
    nb = x_ref.shape[0]

    @pl.when(pl.program_id(0) == 0)
    def _():
        carry_ref[...] = jnp.zeros_like(carry_ref)

    hb = _rms(_to_time_major(x_ref, pad_ref), g_ref[...]).astype(BF16)
    n = hb.shape[0]
    keep = carry_ref.shape[1]

    def branch(w_ref, c_ref, b_ref, slot, cols):
        pre = _dot(hb, w_ref[:, cols])
        out = _causal_conv_tm(pre, carry_ref[slot, :, cols], c_ref, b_ref, nb, cols)
        carry_ref[slot, :, cols] = pre[n - keep:, :]
        return out

    acts = []
    for j in range(wu_ref.shape[1] // FFN_SUB):
        cols = slice(j * FFN_SUB, (j + 1) * FFN_SUB)
        u = branch(wu_ref, cu_ref, bu_ref, 0, cols)
        gt = branch(wg_ref, cg_ref, bg_ref, 1, cols)
        acts.append((_gelu(gt) * u).astype(BF16))
    act = jnp.concatenate(acts, axis=1)
    y = jnp.concatenate([_dot(act, wd_ref[:, j * FFN_SUB:(j + 1) * FFN_SUB])
                         for j in range(wd_ref.shape[1] // FFN_SUB)], axis=1)
    _from_time_major(y, o_ref, pad_ref, res_ref=x_ref)


def _ffn_call(x, layer, g, w_up, w_dw, b_dw, w_down):
    bsz, s, d = x.shape
    d_ff = w_down.shape[1]
    tt = TT_FFN
    width = w_dw.shape[1]
    resident = dict(pipeline_mode=pl.Buffered(1))
    return pl.pallas_call(
        _ffn_kernel,
        grid=(s // tt,),
        in_specs=[
            pl.BlockSpec((bsz, tt, d), lambda i: (0, i, 0)),
            pl.BlockSpec((None, 1, d), lambda i: (layer, 0, 0)),
            pl.BlockSpec((None, d, d_ff), lambda i: (layer, 0, 0), **resident),
            pl.BlockSpec((None, d, d_ff), lambda i: (layer, 0, 1), **resident),
            pl.BlockSpec((None, width, d_ff), lambda i: (layer, 0, 0)),
            pl.BlockSpec((None, width, d_ff), lambda i: (layer, 0, 1)),
            pl.BlockSpec((None, 1, d_ff), lambda i: (layer, 0, 0)),
            pl.BlockSpec((None, 1, d_ff), lambda i: (layer, 0, 1)),
            pl.BlockSpec((None, d_ff, d), lambda i: (layer, 0, 0), **resident),
        ],
        out_specs=pl.BlockSpec((bsz, tt, d), lambda i: (0, i, 0)),
        out_shape=jax.ShapeDtypeStruct(x.shape, F32),
        scratch_shapes=[
            pltpu.VMEM((d // LANES, bsz * (tt + PITCH_PAD), LANES), F32),
            pltpu.VMEM((2, (width - 1) * bsz, d_ff), F32),
        ],
        compiler_params=_cparams(("arbitrary",)),
        name="conv_ffn",
    )(x, g, w_up, w_up, w_dw, w_dw, b_dw, b_dw, w_down)


def _kv_kernel(x_ref, g_ref, wk_ref, wvt_ref, gk_ref, k_ref, vt_ref, *, hd):
    h = _rms(x_ref[...], g_ref[...]).astype(BF16)
    k = _dot(h, wk_ref[...])
    k_ref[...] = _head_rms_rows(k, gk_ref[...], hd).astype(BF16)
    vt = _dot_nt(wvt_ref[...], h)
    ext = hd + BF16_ROWS
    n_blk, _, tq = vt_ref.shape
    ones = jnp.ones((BF16_ROWS, tq), BF16)
    for j in range(n_blk):
        for hh in range(vt.shape[0] // hd):
            vt_ref[j, hh * ext:hh * ext + hd, :] = vt[hh * hd:(hh + 1) * hd, j * tq:(j + 1) * tq].astype(BF16)
            vt_ref[j, hh * ext + hd:(hh + 1) * ext, :] = ones


def _kv_call(x, g, w_k, w_vt, gk_row, hd):
    bsz, s, d = x.shape
    w = w_k.shape[1]
    wx = (w // hd) * (hd + BF16_ROWS)
    ts = TS_KV
    return pl.pallas_call(
        functools.partial(_kv_kernel, hd=hd),
        grid=(bsz, s // ts),
        in_specs=[
            pl.BlockSpec((None, ts, d), lambda b, i: (b, i, 0)),
            pl.BlockSpec((1, d), lambda b, i: (0, 0)),
            pl.BlockSpec((d, w), lambda b, i: (0, 0)),
            pl.BlockSpec((w, d), lambda b, i: (0, 0)),
            pl.BlockSpec((1, w), lambda b, i: (0, 0)),
        ],
        out_specs=[
            pl.BlockSpec((None, ts, w), lambda b, i: (b, i, 0)),
            pl.BlockSpec((None, ts // TQ, wx, TQ), lambda b, i: (b, i, 0, 0)),
        ],
        out_shape=[jax.ShapeDtypeStruct((bsz, s, w), BF16),
                   jax.ShapeDtypeStruct((bsz, s // TQ, wx, TQ), BF16)],
        compiler_params=_cparams(("arbitrary", "arbitrary")),
        name="kv_proj",
    )(x, g, w_k, w_vt, gk_row)


def _bias_kernel(f_ref, o_ref):
    x = jnp.broadcast_to(f_ref[0], (CHUNK, KEY_ROWS))
    r = pltpu.roll(x, 0, 1, stride=1, stride_axis=0)
    band = r.T[CHUNK - 1:CHUNK - 1 + BAND, :]
    negs = jnp.full((CHUNK, CHUNK), NEG, F32)
    lo = jnp.concatenate([band, negs], axis=0)
    hi = jnp.concatenate([negs, band], axis=0)
    o_ref[0, 0:NEG_ROWS, :] = jnp.full((NEG_ROWS, 2 * CHUNK), NEG, F32)
    o_ref[0, NEG_ROWS:, :] = jnp.concatenate([lo, hi], axis=1)


def _bias_call(rel_bias):
    n_heads, n_rel = rel_bias.shape
    assert n_rel == MAX_REL + CHUNK and 2 * CHUNK == LANES
    f = jnp.concatenate([jnp.broadcast_to(rel_bias[:, n_rel - 1:], (n_heads, KEY_ROWS - 1 - n_rel)),
                         rel_bias[:, ::-1], rel_bias[:, :1]], axis=1) * LOG2E
    rows = NEG_ROWS + KEY_ROWS
    return pl.pallas_call(
        _bias_kernel,
        grid=(n_heads,),
        in_specs=[pl.BlockSpec((1, 1, KEY_ROWS), lambda h: (h, 0, 0))],
        out_specs=pl.BlockSpec((1, rows, LANES), lambda h: (h, 0, 0)),
        out_shape=jax.ShapeDtypeStruct((n_heads, rows, LANES), F32),
        compiler_params=_cparams(("arbitrary",)),
        name="band_bias",
    )(f.astype(F32)[:, None, :])


def _mixer_b_kernel(x_ref, g_ref, win_ref, gqa_ref, k0_ref, k1_ref, k2_ref, v0_ref, v1_ref, v2_ref,
                    pb_ref, km_ref, vm_ref, gq_ref, wout_ref, o_ref, qa_ref, s_ref, p_ref, att_ref,
                    *, att_w, att_hd, n_heads, hd):
    i = pl.program_id(1)
    x = x_ref[...]
    tq = x.shape[0]
    half = tq // 2
    assert half == LANES
    h = _rms(x, g_ref[...]).astype(BF16)
    proj = _dot(h, win_ref[...])
    q_t = proj[:, :att_w].T
    qn = []
    for hh in range(att_w // att_hd):
        q_h = q_t[hh * att_hd:(hh + 1) * att_hd]
        ms = jnp.mean(q_h * q_h, axis=0, keepdims=True)
        qn.append(q_h * lax.rsqrt(ms + EPS))
    qa_ref[...] = pltpu.bitcast((jnp.concatenate(qn, axis=0) * gqa_ref[...]).astype(BF16), jnp.uint32)
    qm = proj[:, att_w:]

    k_refs = (k0_ref, k1_ref, k2_ref)
    v_refs = (v0_ref, v1_ref, v2_ref)
    ok0 = i >= 2
    ok1 = i >= 1
    ext = att_hd + BF16_ROWS

    zblk = jnp.zeros((half, half), BF16)
    for slot in range(2):
        p_ref[slot, 0, 0:half, half:] = zblk
        p_ref[slot, 2, half:, 0:half] = zblk

    def bias_rows(hh, ok, start, size):
        first = pl.multiple_of(jnp.where(ok, NEG_ROWS + start, 0), LANES)
        return pb_ref[hh, pl.ds(first, size), :]

    def scores(hh):
        slot = hh % 2
        col = (hh // 2) * LANES
        rows = att_hd // 2
        q_t = pltpu.bitcast(qa_ref[hh * rows:(hh + 1) * rows, :], BF16)
        qz = jnp.concatenate([q_t, jnp.zeros_like(q_t)] if hh % 2 == 0 else [jnp.zeros_like(q_t), q_t], axis=0)
        st = [_dot(kr[:, col:col + LANES], qz) for kr in k_refs]
        s_ref[slot, 0, 0:tq, :] = st[0][:, :half] + bias_rows(hh, ok0, 0, tq)
        s_ref[slot, 0, tq:2 * tq, :] = st[1][:, :half] + bias_rows(hh, ok1, tq, tq)
        s_ref[slot, 0, 2 * tq:, :] = st[2][:half, :half] + pb_ref[hh, NEG_ROWS + 2 * tq:, :]
        s_ref[slot, 1, 0:half, :] = st[0][half:, half:] + bias_rows(hh, ok0, 0, half)
        s_ref[slot, 1, half:half + tq, :] = st[1][:, half:] + bias_rows(hh, ok1, half, tq)
        s_ref[slot, 1, half + tq:, :] = st[2][:, half:] + pb_ref[hh, NEG_ROWS + half + tq:, :]

    def probs(hh):
        slot = hh % 2
        z = s_ref[slot, 0]
        e = jnp.exp2(z - jnp.max(z, axis=0, keepdims=True)).astype(BF16)
        p_ref[slot, 0, :, 0:half] = e[0:tq]
        p_ref[slot, 1, :, 0:half] = e[tq:2 * tq]
        p_ref[slot, 2, 0:half, 0:half] = e[2 * tq:]
        z = s_ref[slot, 1]
        e = jnp.exp2(z - jnp.max(z, axis=0, keepdims=True)).astype(BF16)
        p_ref[slot, 0, half:, half:] = e[0:half]
        p_ref[slot, 1, :, half:] = e[half:half + tq]
        p_ref[slot, 2, :, half:] = e[half + tq:]

    def values(hh):
        slot = hh % 2
        vt_h = jnp.concatenate([vr[hh * ext:(hh + 1) * ext, :] for vr in v_refs], axis=1)
        ot = _dot(vt_h, p_ref[slot].reshape(len(v_refs) * tq, tq))
        att_ref[hh * att_hd:(hh + 1) * att_hd, :] = ot[:att_hd] / ot[att_hd:att_hd + 1]

    n_att = att_w // att_hd
    for step in range(n_att + 2):
        if 0 <= step - 2:
            values(step - 2)
        if 0 <= step - 1 < n_att:
            probs(step - 1)
        if step < n_att:
            scores(step)

    att = att_ref[...].T.astype(BF16)
    mo = _mem_attention(qm, km_ref, vm_ref, gq_ref[...], n_heads, hd).astype(BF16)
    y = _dot(att, wout_ref[:att_w, :]) + _dot(mo, wout_ref[att_w:, :])
    o_ref[...] = x + y


def _mixer_b_call(x, g, w_in, gqa_row, k, vt, pb, km, vm, gq, w_out, att_hd, n_heads, hd):
    bsz, s, d = x.shape
    att_w = k.shape[2]
    vt_w = vt.shape[2]
    n_mem, mw = km.shape[1:]
    tq = TQ
    const = lambda *shape: pl.BlockSpec(shape, lambda b, i: (0,) * len(shape))
    kspec = lambda r: pl.BlockSpec((None, tq, att_w), lambda b, i: (b, jnp.maximum(i - 2 + r, 0), 0))
    vspec = lambda r: pl.BlockSpec((None, None, vt_w, tq), lambda b, i: (b, jnp.maximum(i - 2 + r, 0), 0, 0))
    return pl.pallas_call(
        functools.partial(_mixer_b_kernel, att_w=att_w, att_hd=att_hd, n_heads=n_heads, hd=hd),
        grid=(bsz, s // tq),
        in_specs=[
            pl.BlockSpec((None, tq, d), lambda b, i: (b, i, 0)),
            const(1, d),
            const(*w_in.shape),
            const(att_w, tq),
            kspec(0), kspec(1), kspec(2),
            vspec(0), vspec(1), vspec(2),
            const(*pb.shape),
            pl.BlockSpec((None, n_mem, mw), lambda b, i: (b, 0, 0)),
            pl.BlockSpec((None, n_mem, mw), lambda b, i: (b, 0, 0)),
            const(1, hd),
            const(*w_out.shape),
        ],
        out_specs=pl.BlockSpec((None, tq, d), lambda b, i: (b, i, 0)),
        out_shape=jax.ShapeDtypeStruct(x.shape, F32),
        scratch_shapes=[
            pltpu.VMEM((att_w // 2, tq), jnp.uint32),
            pltpu.VMEM((2, 2, KEY_ROWS, LANES), F32),
            pltpu.VMEM((2, 3, tq, tq), BF16),
            pltpu.VMEM((att_w, tq), F32),
        ],
        compiler_params=_cparams(("arbitrary", "arbitrary")),
        name="mixer_b",
    )(x, g, w_in, gqa_row, k, k, k, vt, vt, vt, pb, km, vm, gq, w_out)


def _block_diag_windows(w_r, w_i):
    n, bw, _ = w_r.shape
    width = n * bw
    assert width % MXU_COLS == 0
    reps = -(-MXU_COLS // bw) + 1
    starts, tiles = [], []
    for j in range(width // MXU_COLS):
        c0 = j * MXU_COLS
        lo = (c0 // bw) * bw
        hi = -((-(c0 + MXU_COLS)) // bw) * bw
        ws = min((lo // LANES) * LANES, width - BD_WINDOW)
        assert ws <= lo and hi <= ws + BD_WINDOW
        rows = ws + jnp.arange(BD_WINDOW)[:, None]
        cols = c0 + jnp.arange(MXU_COLS)[None, :]
        on_diag = (rows // bw) == (cols // bw)

        def window(w):
            rep = jnp.tile(w.reshape(width, bw)[ws:ws + BD_WINDOW], (1, reps))[:, c0 % bw:c0 % bw + MXU_COLS]
            return jnp.where(on_diag, rep, 0.0)

        tiles.append(jnp.concatenate([window(w_r), window(w_i)], axis=1))
        starts.append(ws)
    return jnp.stack(tiles).astype(BF16), tuple(starts)


def kernel(x, mem, g_mix, g_mem, w_mem_kv, g_q_mem, g_k_mem, w_in_a, w_conv_a, b_conv_a, w_r_a, b_r_a,
           w_i_a, b_i_a, lam_a, w_out_a, g_kv, w_kv, g_k_att, w_in_b, g_q_att, rel_bias_b, w_out_b,
           g_ffn, w_up, w_dw_ffn, b_dw_ffn, w_down):
    bsz, s, d = x.shape
    depth = g_mix.shape[0]
    n_a = w_in_a.shape[0]
    mem_hd = g_q_mem.shape[1]
    mem_w = w_mem_kv.shape[2] // 2
    mem_heads = mem_w // mem_hd
    att_hd = g_k_att.shape[0]
    att_w = w_kv.shape[1] // 2
    att_heads = att_w // att_hd
    d_rnn = w_conv_a.shape[2]
    assert att_hd * 2 == LANES and att_hd == CHUNK and bsz == SUBLANES
    assert s % TS_KV == 0 and s % TS_MIX == 0 and s % TQ == 0 and s % TT_FFN == 0 and s % TT_RGLRU == 0

    row = lambda v: v.reshape(1, -1).astype(F32)
    km_all, vm_all = _memkv_call(mem, g_mem, w_mem_kv, g_k_mem, mem_heads, mem_hd)

    g_ffn3 = g_ffn[:, None, :].astype(F32)
    b_dw3 = b_dw_ffn[:, None, :].astype(F32)
    w_up_bf = w_up.astype(BF16)
    w_down_bf = w_down.astype(BF16)
    k = vt = None
    for l in range(depth):
        if l < n_a:
            wbd, bd_starts = _block_diag_windows(w_r_a[l], w_i_a[l])
            w_in = w_in_a[l].astype(BF16)
            main = _rglru_call(x, row(g_mix[l]), w_in[:, :2 * d_rnn], w_conv_a[l], row(b_conv_a[l]), wbd,
                               bd_starts, row(b_r_a[l]), row(b_i_a[l]), row(lam_a[l]))
            x = _mixout_call(x, main, row(g_mix[l]), w_in[:, 2 * d_rnn:], km_all[l], vm_all[l],
                             row(g_q_mem[l]), w_out_a[l].astype(BF16), mem_heads, mem_hd)
        else:
            j = l - n_a
            if j == 0:
                k, vt = _kv_call(x, row(g_kv), w_kv[:, :att_w].astype(BF16), w_kv[:, att_w:].T.astype(BF16),
                                 row(jnp.tile(g_k_att, att_heads)), att_hd)
            gqa = jnp.broadcast_to((jnp.tile(g_q_att[j], att_heads) * (att_hd ** -0.5 * LOG2E))[:, None],
                                   (att_w, TQ)).astype(F32)
            x = _mixer_b_call(x, row(g_mix[l]), w_in_b[j].astype(BF16), gqa, k, vt, _bias_call(rel_bias_b[j]),
                              km_all[l], vm_all[l], row(g_q_mem[l]), w_out_b[j].astype(BF16),
                              att_hd, mem_heads, mem_hd)
        x = _ffn_call(x, l, g_ffn3, w_up_bf, w_dw_ffn, b_dw3, w_down_bf)
    return x
```

```python
import functools
import math

import jax
import jax.numpy as jnp
from jax import lax
from jax.experimental import pallas as pl
from jax.experimental.pallas import tpu as pltpu

EPS = 1e-6
CHUNK = 64
N_PREV_CHUNKS = 8
BAND = (N_PREV_CHUNKS + 1) * CHUNK
MAX_REL = 256
RG_C = 8.0
NEG = -1e30
LOG2E = math.log2(math.e)

LANES = 128
SUBLANES = 8
BF16_ROWS = 16
PITCH_PAD = 4
MXU_COLS = 256
BD_WINDOW = 512
VMEM_LIMIT = 56 * 1024 * 1024

TT_RGLRU = 32
TT_FFN = 128
FFN_SUB = 256
TS_MIX = 1024
TS_KV = 1024
TQ = 4 * CHUNK
KEY_ROWS = BAND + CHUNK
NEG_ROWS = TQ

F32 = jnp.float32
BF16 = jnp.bfloat16


def _cparams(sem):
    return pltpu.CompilerParams(dimension_semantics=sem, vmem_limit_bytes=VMEM_LIMIT)


def _dot(a, b):
    return jnp.dot(a, b, preferred_element_type=F32)


def _dot_nt(a, b):
    return lax.dot_general(a, b, (((1,), (1,)), ((), ())), preferred_element_type=F32)


def _rms(x, g):
    ms = jnp.mean(x * x, axis=-1, keepdims=True)
    return x * lax.rsqrt(ms + EPS) * g


def _gelu(x):
    c = -2.0 * 0.7978845608028654 * LOG2E
    return x / (1.0 + jnp.exp2(x * (c + (c * 0.044715) * (x * x))))


def _gelu_tanh(x):
    c = 0.7978845608028654
    return x * (0.5 + 0.5 * jnp.tanh(x * (c + (c * 0.044715) * (x * x))))


def _sigmoid(x):
    return 0.5 + 0.5 * jnp.tanh(0.5 * x)


def _to_time_major(x_ref, pad_ref):
    nb, tt, d = x_ref.shape
    pitch = pad_ref.shape[1] // nb
    for b in range(nb):
        for c in range(d // LANES):
            pad_ref[c, b * pitch:b * pitch + tt, :] = x_ref[b, :, c * LANES:(c + 1) * LANES].astype(F32)
    rows = []
    for t in range(tt):
        rows.append(jnp.concatenate(
            [pad_ref[c, pl.ds(t, nb, stride=pitch), :] for c in range(d // LANES)], axis=1))
    return jnp.concatenate(rows, axis=0)


def _from_time_major(y, o_ref, pad_ref, res_ref=None):
    nb, tt, d = o_ref.shape
    pitch = pad_ref.shape[1] // nb
    for t in range(tt):
        for c in range(d // LANES):
            pad_ref[c, pl.ds(t, nb, stride=pitch), :] = y[t * nb:(t + 1) * nb, c * LANES:(c + 1) * LANES]
    for b in range(nb):
        for c in range(d // LANES):
            v = pad_ref[c, b * pitch:b * pitch + tt, :]
            if res_ref is not None:
                v = v + res_ref[b, :, c * LANES:(c + 1) * LANES]
            o_ref[b, :, c * LANES:(c + 1) * LANES] = v.astype(o_ref.dtype)


def _causal_conv_tm(x, prev, w_ref, b_ref, nb, cols=slice(None)):
    n = x.shape[0]
    width = w_ref.shape[0]
    xc = jnp.concatenate([prev, x], axis=0)
    out = x * w_ref[width - 1:width, cols] + b_ref[:, cols]
    for k in range(width - 1):
        out = out + xc[k * nb:k * nb + n] * w_ref[k:k + 1, cols]
    return out


def _head_rms_rows(x, g, hd):
    t, w = x.shape
    lane = lax.broadcasted_iota(jnp.int32, (t, LANES), 1)
    lo = lane < hd
    outs = []
    for p in range(w // LANES):
        xp = x[:, p * LANES:(p + 1) * LANES]
        x2 = xp * xp
        s_lo = jnp.sum(jnp.where(lo, x2, 0.0), axis=-1, keepdims=True)
        s_hi = jnp.sum(jnp.where(lo, 0.0, x2), axis=-1, keepdims=True)
        r_lo = lax.rsqrt(s_lo * (1.0 / hd) + EPS)
        r_hi = lax.rsqrt(s_hi * (1.0 / hd) + EPS)
        outs.append(xp * jnp.where(lo, r_lo, r_hi))
    return jnp.concatenate(outs, axis=-1) * g


def _mem_attention(qm, km_ref, vm_ref, gq, n_heads, hd):
    outs = []
    for hh in range(n_heads):
        sl = slice(hh * hd, (hh + 1) * hd)
        qn = (_rms(qm[:, sl], gq) * (hd ** -0.5)).astype(BF16)
        s = _dot_nt(qn, km_ref[:, sl])
        m = jnp.max(s, axis=-1, keepdims=True)
        e = jnp.exp(s - m)
        l = jnp.sum(e, axis=-1, keepdims=True)
        o = _dot(e.astype(BF16), vm_ref[:, sl])
        outs.append(o / l)
    return jnp.concatenate(outs, axis=-1)


def _memkv_kernel(mem_ref, g_ref, w_ref, gk_ref, km_ref, vm_ref, *, n_heads, hd):
    bsz, n_mem, d = mem_ref.shape
    mn = _rms(mem_ref[...].reshape(bsz * n_mem, d), g_ref[...]).astype(BF16)
    kv = _dot(mn, w_ref[...])
    w = n_heads * hd
    km = jnp.concatenate([_rms(kv[:, hh * hd:(hh + 1) * hd], gk_ref[...]) for hh in range(n_heads)], axis=-1)
    km_ref[...] = km.astype(BF16).reshape(bsz, n_mem, w)
    vm_ref[...] = kv[:, w:].astype(BF16).reshape(bsz, n_mem, w)


def _memkv_call(mem, g_mem, w_mem_kv, g_k_mem, n_heads, hd):
    depth = g_mem.shape[0]
    bsz, n_mem, d = mem.shape
    w = n_heads * hd
    out = jax.ShapeDtypeStruct((depth, bsz, n_mem, w), BF16)
    return pl.pallas_call(
        functools.partial(_memkv_kernel, n_heads=n_heads, hd=hd),
        grid=(depth,),
        in_specs=[
            pl.BlockSpec((bsz, n_mem, d), lambda l: (0, 0, 0)),
            pl.BlockSpec((None, 1, d), lambda l: (l, 0, 0)),
            pl.BlockSpec((None, d, 2 * w), lambda l: (l, 0, 0)),
            pl.BlockSpec((None, 1, hd), lambda l: (l, 0, 0)),
        ],
        out_specs=[
            pl.BlockSpec((None, bsz, n_mem, w), lambda l: (l, 0, 0, 0)),
            pl.BlockSpec((None, bsz, n_mem, w), lambda l: (l, 0, 0, 0)),
        ],
        out_shape=[out, out],
        compiler_params=_cparams(("arbitrary",)),
        name="mem_kv",
    )(mem, g_mem.reshape(depth, 1, d), w_mem_kv.astype(BF16), g_k_mem.reshape(depth, 1, hd))


def _rglru_kernel(x_ref, g_ref, win_ref, wc_ref, bc_ref, wbd_ref, br_ref, bi_ref, lam_ref, o_ref,
                  pad_ref, xprev_ref, hprev_ref, *, bd_starts):
    @pl.when(pl.program_id(0) == 0)
    def _():
        xprev_ref[...] = jnp.zeros_like(xprev_ref)
        hprev_ref[...] = jnp.zeros_like(hprev_ref)

    nb, tt, _ = x_ref.shape
    d_rnn = wc_ref.shape[1]
    x = _to_time_major(x_ref, pad_ref)
    n = x.shape[0]
    h = _rms(x, g_ref[...]).astype(BF16)
    proj = _dot(h, win_ref[...])
    xr = proj[:, :d_rnn]
    yg = proj[:, d_rnn:]

    conv = _causal_conv_tm(xr, xprev_ref[...], wc_ref, bc_ref, nb)
    xprev_ref[...] = xr[n - xprev_ref.shape[0]:, :]

    cb = conv.astype(BF16)
    win = wbd_ref.shape[1]
    ris = [_dot(cb[:, ws:ws + win], wbd_ref[j]) for j, ws in enumerate(bd_starts)]
    r = _sigmoid(jnp.concatenate([ri[:, :MXU_COLS] for ri in ris], axis=-1) + br_ref[...])
    ig = _sigmoid(jnp.concatenate([ri[:, MXU_COLS:] for ri in ris], axis=-1) + bi_ref[...])
    lam = lam_ref[...]
    neg_c = -RG_C * (jnp.minimum(lam, 0.0) - jnp.log1p(jnp.exp(-jnp.abs(lam))))
    neg_log_a = r * neg_c
    a = jnp.exp2(r * (neg_c * -LOG2E))
    z = jnp.tanh(neg_log_a) * (a * a + 1.0)
    u = jnp.where(z > 0.0, z * lax.rsqrt(z), 0.0) * ig * conv

    hcur = hprev_ref[...]
    hs = []
    for t in range(tt):
        hcur = a[t * nb:(t + 1) * nb] * hcur + u[t * nb:(t + 1) * nb]
        hs.append(hcur)
    hprev_ref[...] = hcur
    main = jnp.concatenate(hs, axis=0) * _gelu_tanh(yg)
    _from_time_major(main, o_ref, pad_ref)


def _rglru_call(x, g, w_in_xy, w_conv, b_conv, wbd, bd_starts, b_r, b_i, lam):
    bsz, s, d = x.shape
    d_rnn = w_conv.shape[1]
    tt = TT_RGLRU
    const = lambda *shape: pl.BlockSpec(shape, lambda i: (0,) * len(shape))
    return pl.pallas_call(
        functools.partial(_rglru_kernel, bd_starts=bd_starts),
        grid=(s // tt,),
        in_specs=[
            pl.BlockSpec((bsz, tt, d), lambda i: (0, i, 0)),
            const(1, d),
            const(*w_in_xy.shape),
            const(*w_conv.shape),
            const(1, d_rnn),
            const(*wbd.shape),
            const(1, d_rnn),
            const(1, d_rnn),
            const(1, d_rnn),
        ],
        out_specs=pl.BlockSpec((bsz, tt, d_rnn), lambda i: (0, i, 0)),
        out_shape=jax.ShapeDtypeStruct((bsz, s, d_rnn), BF16),
        scratch_shapes=[
            pltpu.VMEM((d_rnn // LANES, bsz * (tt + PITCH_PAD), LANES), F32),
            pltpu.VMEM(((w_conv.shape[0] - 1) * bsz, d_rnn), F32),
            pltpu.VMEM((bsz, d_rnn), F32),
        ],
        compiler_params=_cparams(("arbitrary",)),
        name="rglru_a",
    )(x, g, w_in_xy, w_conv, b_conv, wbd, b_r, b_i, lam)


def _mixout_kernel(x_ref, main_ref, g_ref, wq_ref, km_ref, vm_ref, gq_ref, wout_ref, o_ref, *, n_heads, hd):
    x = x_ref[...]
    d_main = main_ref.shape[1]
    h = _rms(x, g_ref[...]).astype(BF16)
    qm = _dot(h, wq_ref[...])
    mo = _mem_attention(qm, km_ref, vm_ref, gq_ref[...], n_heads, hd).astype(BF16)
    y = _dot(main_ref[...], wout_ref[:d_main, :]) + _dot(mo, wout_ref[d_main:, :])
    o_ref[...] = x_ref[...] + y


def _mixout_call(x, main, g, w_q, km, vm, gq, w_out, n_heads, hd):
    bsz, s, d = x.shape
    d_main = main.shape[2]
    n_mem, mw = km.shape[1:]
    ts = TS_MIX
    const = lambda *shape: pl.BlockSpec(shape, lambda b, i: (0,) * len(shape))
    return pl.pallas_call(
        functools.partial(_mixout_kernel, n_heads=n_heads, hd=hd),
        grid=(bsz, s // ts),
        in_specs=[
            pl.BlockSpec((None, ts, d), lambda b, i: (b, i, 0)),
            pl.BlockSpec((None, ts, d_main), lambda b, i: (b, i, 0)),
            const(1, d),
            const(*w_q.shape),
            pl.BlockSpec((None, n_mem, mw), lambda b, i: (b, 0, 0)),
            pl.BlockSpec((None, n_mem, mw), lambda b, i: (b, 0, 0)),
            const(1, hd),
            const(*w_out.shape),
        ],
        out_specs=pl.BlockSpec((None, ts, d), lambda b, i: (b, i, 0)),
        out_shape=jax.ShapeDtypeStruct(x.shape, F32),
        compiler_params=_cparams(("arbitrary", "arbitrary")),
        name="mixout_a",
    )(x, main, g, w_q, km, vm, gq, w_out)


def _ffn_kernel(x_ref, g_ref, wu_ref, wg_ref, cu_ref, cg_ref, bu_ref, bg_ref, wd_ref, o_ref,
                pad_ref, carry_ref):
    nb = x_ref.shape[0]

    @pl.when(pl.program_id(0) == 0)
    def _():
        carry_ref[...] = jnp.zeros_like(carry_ref)

    hb = _rms(_to_time_major(x_ref, pad_ref), g_ref[...]).astype(BF16)
    n = hb.shape[0]
    keep = carry_ref.shape[1]

    def branch(w_ref, c_ref, b_ref, slot, cols):
        pre = _dot(hb, w_ref[:, cols])
        out = _causal_conv_tm(pre, carry_ref[slot, :, cols], c_ref, b_ref, nb, cols)
        carry_ref[slot, :, cols] = pre[n - keep:, :]
        return out

    acts = []
    for j in range(wu_ref.shape[1] // FFN_SUB):
        cols = slice(j * FFN_SUB, (j + 1) * FFN_SUB)
        u = branch(wu_ref, cu_ref, bu_ref, 0, cols)
        gt = branch(wg_ref, cg_ref, bg_ref, 1, cols)
        acts.append((_gelu(gt) * u).astype(BF16))
    act = jnp.concatenate(acts, axis=1)
    y = jnp.concatenate([_dot(act, wd_ref[:, j * FFN_SUB:(j + 1) * FFN_SUB])
                         for j in range(wd_ref.shape[1] // FFN_SUB)], axis=1)
    _from_time_major(y, o_ref, pad_ref, res_ref=x_ref)


def _ffn_call(x, layer, g, w_up, w_dw, b_dw, w_down):
    bsz, s, d = x.shape
    d_ff = w_down.shape[1]
    tt = TT_FFN
    width = w_dw.shape[1]
    resident = dict(pipeline_mode=pl.Buffered(1))
    return pl.pallas_call(
        _ffn_kernel,
        grid=(s // tt,),
        in_specs=[
            pl.BlockSpec((bsz, tt, d), lambda i: (0, i, 0)),
            pl.BlockSpec((None, 1, d), lambda i: (layer, 0, 0)),
            pl.BlockSpec((None, d, d_ff), lambda i: (layer, 0, 0), **resident),
            pl.BlockSpec((None, d, d_ff), lambda i: (layer, 0, 1), **resident),
            pl.BlockSpec((None, width, d_ff), lambda i: (layer, 0, 0)),
            pl.BlockSpec((None, width, d_ff), lambda i: (layer, 0, 1)),
            pl.BlockSpec((None, 1, d_ff), lambda i: (layer, 0, 0)),
            pl.BlockSpec((None, 1, d_ff), lambda i: (layer, 0, 1)),
            pl.BlockSpec((None, d_ff, d), lambda i: (layer, 0, 0), **resident),
        ],
        out_specs=pl.BlockSpec((bsz, tt, d), lambda i: (0, i, 0)),
        out_shape=jax.ShapeDtypeStruct(x.shape, F32),
        scratch_shapes=[
            pltpu.VMEM((d // LANES, bsz * (tt + PITCH_PAD), LANES), F32),
            pltpu.VMEM((2, (width - 1) * bsz, d_ff), F32),
        ],
        compiler_params=_cparams(("arbitrary",)),
        name="conv_ffn",
    )(x, g, w_up, w_up, w_dw, w_dw, b_dw, b_dw, w_down)


def _kv_kernel(x_ref, g_ref, wk_ref, wvt_ref, gk_ref, k_ref, vt_ref, *, hd):
    h = _rms(x_ref[...], g_ref[...]).astype(BF16)
    k = _dot(h, wk_ref[...])
    k_ref[...] = _head_rms_rows(k, gk_ref[...], hd).astype(BF16)
    vt = _dot_nt(wvt_ref[...], h)
    ext = hd + BF16_ROWS
    n_blk, _, tq = vt_ref.shape
    ones = jnp.ones((BF16_ROWS, tq), BF16)
    for j in range(n_blk):
        for hh in range(vt.shape[0] // hd):
            vt_ref[j, hh * ext:hh * ext + hd, :] = vt[hh * hd:(hh + 1) * hd, j * tq:(j + 1) * tq].astype(BF16)
            vt_ref[j, hh * ext + hd:(hh + 1) * ext, :] = ones


def _kv_call(x, g, w_k, w_vt, gk_row, hd):
    bsz, s, d = x.shape
    w = w_k.shape[1]
    wx = (w // hd) * (hd + BF16_ROWS)
    ts = TS_KV
    return pl.pallas_call(
        functools.partial(_kv_kernel, hd=hd),
        grid=(bsz, s // ts),
        in_specs=[
            pl.BlockSpec((None, ts, d), lambda b, i: (b, i, 0)),
            pl.BlockSpec((1, d), lambda b, i: (0, 0)),
            pl.BlockSpec((d, w), lambda b, i: (0, 0)),
            pl.BlockSpec((w, d), lambda b, i: (0, 0)),
            pl.BlockSpec((1, w), lambda b, i: (0, 0)),
        ],
        out_specs=[
            pl.BlockSpec((None, ts, w), lambda b, i: (b, i, 0)),
            pl.BlockSpec((None, ts // TQ, wx, TQ), lambda b, i: (b, i, 0, 0)),
        ],
        out_shape=[jax.ShapeDtypeStruct((bsz, s, w), BF16),
                   jax.ShapeDtypeStruct((bsz, s // TQ, wx, TQ), BF16)],
        compiler_params=_cparams(("arbitrary", "arbitrary")),
        name="kv_proj",
    )(x, g, w_k, w_vt, gk_row)


def _bias_kernel(f_ref, o_ref):
    x = jnp.broadcast_to(f_ref[0], (CHUNK, KEY_ROWS))
    r = pltpu.roll(x, 0, 1, stride=1, stride_axis=0)
    band = r.T[CHUNK - 1:CHUNK - 1 + BAND, :]
    negs = jnp.full((CHUNK, CHUNK), NEG, F32)
    lo = jnp.concatenate([band, negs], axis=0)
    hi = jnp.concatenate([negs, band], axis=0)
    o_ref[0, 0:NEG_ROWS, :] = jnp.full((NEG_ROWS, 2 * CHUNK), NEG, F32)
    o_ref[0, NEG_ROWS:, :] = jnp.concatenate([lo, hi], axis=1)


def _bias_call(rel_bias):
    n_heads, n_rel = rel_bias.shape
    assert n_rel == MAX_REL + CHUNK and 2 * CHUNK == LANES
    f = jnp.concatenate([jnp.broadcast_to(rel_bias[:, n_rel - 1:], (n_heads, KEY_ROWS - 1 - n_rel)),
                         rel_bias[:, ::-1], rel_bias[:, :1]], axis=1) * LOG2E
    rows = NEG_ROWS + KEY_ROWS
    return pl.pallas_call(
        _bias_kernel,
        grid=(n_heads,),
        in_specs=[pl.BlockSpec((1, 1, KEY_ROWS), lambda h: (h, 0, 0))],
        out_specs=pl.BlockSpec((1, rows, LANES), lambda h: (h, 0, 0)),
        out_shape=jax.ShapeDtypeStruct((n_heads, rows, LANES), F32),
        compiler_params=_cparams(("arbitrary",)),
        name="band_bias",
    )(f.astype(F32)[:, None, :])


def _mixer_b_kernel(x_ref, g_ref, win_ref, gqa_ref, k0_ref, k1_ref, k2_ref, v0_ref, v1_ref, v2_ref,
                    pb_ref, km_ref, vm_ref, gq_ref, wout_ref, o_ref, qa_ref, s_ref, p_ref, att_ref,
                    *, att_w, att_hd, n_heads, hd):
    i = pl.program_id(1)
    x = x_ref[...]
    tq = x.shape[0]
    half = tq // 2
    assert half == LANES
    h = _rms(x, g_ref[...]).astype(BF16)
    proj = _dot(h, win_ref[...])
    q_t = proj[:, :att_w].T
    qn = []
    for hh in range(att_w // att_hd):
        q_h = q_t[hh * att_hd:(hh + 1) * att_hd]
        ms = jnp.mean(q_h * q_h, axis=0, keepdims=True)
        qn.append(q_h * lax.rsqrt(ms + EPS))
    qa_ref[...] = pltpu.bitcast((jnp.concatenate(qn, axis=0) * gqa_ref[...]).astype(BF16), jnp.uint32)
    qm = proj[:, att_w:]

    k_refs = (k0_ref, k1_ref, k2_ref)
    v_refs = (v0_ref, v1_ref, v2_ref)
    ok0 = i >= 2
    ok1 = i >= 1
    ext = att_hd + BF16_ROWS

    zblk = jnp.zeros((half, half), BF16)
    for slot in range(2):
        p_ref[slot, 0, 0:half, half:] = zblk
        p_ref[slot, 2, half:, 0:half] = zblk

    def bias_rows(hh, ok, start, size):
        first = pl.multiple_of(jnp.where(ok, NEG_ROWS + start, 0), LANES)
        return pb_ref[hh, pl.ds(first, size), :]

    def scores(hh):
        slot = hh % 2
        col = (hh // 2) * LANES
        rows = att_hd // 2
        q_t = pltpu.bitcast(qa_ref[hh * rows:(hh + 1) * rows, :], BF16)
        qz = jnp.concatenate([q_t, jnp.zeros_like(q_t)] if hh % 2 == 0 else [jnp.zeros_like(q_t), q_t], axis=0)
        st = [_dot(kr[:, col:col + LANES], qz) for kr in k_refs]
        s_ref[slot, 0, 0:tq, :] = st[0][:, :half] + bias_rows(hh, ok0, 0, tq)
        s_ref[slot, 0, tq:2 * tq, :] = st[1][:, :half] + bias_rows(hh, ok1, tq, tq)
        s_ref[slot, 0, 2 * tq:, :] = st[2][:half, :half] + pb_ref[hh, NEG_ROWS + 2 * tq:, :]
        s_ref[slot, 1, 0:half, :] = st[0][half:, half:] + bias_rows(hh, ok0, 0, half)
        s_ref[slot, 1, half:half + tq, :] = st[1][:, half:] + bias_rows(hh, ok1, half, tq)
        s_ref[slot, 1, half + tq:, :] = st[2][:, half:] + pb_ref[hh, NEG_ROWS + half + tq:, :]

    def probs(hh):
        slot = hh % 2
        z = s_ref[slot, 0]
        e = jnp.exp2(z - jnp.max(z, axis=0, keepdims=True)).astype(BF16)
        p_ref[slot, 0, :, 0:half] = e[0:tq]
        p_ref[slot, 1, :, 0:half] = e[tq:2 * tq]
        p_ref[slot, 2, 0:half, 0:half] = e[2 * tq:]
        z = s_ref[slot, 1]
        e = jnp.exp2(z - jnp.max(z, axis=0, keepdims=True)).astype(BF16)
        p_ref[slot, 0, half:, half:] = e[0:half]
        p_ref[slot, 1, :, half:] = e[half:half + tq]
        p_ref[slot, 2, :, half:] = e[half + tq:]

    def values(hh):
        slot = hh % 2
        vt_h = jnp.concatenate([vr[hh * ext:(hh + 1) * ext, :] for vr in v_refs], axis=1)
        ot = _dot(vt_h, p_ref[slot].reshape(len(v_refs) * tq, tq))
        att_ref[hh * att_hd:(hh + 1) * att_hd, :] = ot[:att_hd] / ot[att_hd:att_hd + 1]

    n_att = att_w // att_hd
    for step in range(n_att + 2):
        if 0 <= step - 2:
            values(step - 2)
        if 0 <= step - 1 < n_att:
            probs(step - 1)
        if step < n_att:
            scores(step)

    att = att_ref[...].T.astype(BF16)
    mo = _mem_attention(qm, km_ref, vm_ref, gq_ref[...], n_heads, hd).astype(BF16)
    y = _dot(att, wout_ref[:att_w, :]) + _dot(mo, wout_ref[att_w:, :])
    o_ref[...] = x_ref[...] + y


def _mixer_b_call(x, g, w_in, gqa_row, k, vt, pb, km, vm, gq, w_out, att_hd, n_heads, hd):
    bsz, s, d = x.shape
    att_w = k.shape[2]
    vt_w = vt.shape[2]
    n_mem, mw = km.shape[1:]
    tq = TQ
    const = lambda *shape: pl.BlockSpec(shape, lambda b, i: (0,) * len(shape))
    kspec = lambda r: pl.BlockSpec((None, tq, att_w), lambda b, i: (b, jnp.maximum(i - 2 + r, 0), 0))
    vspec = lambda r: pl.BlockSpec((None, None, vt_w, tq), lambda b, i: (b, jnp.maximum(i - 2 + r, 0), 0, 0))
    return pl.pallas_call(
        functools.partial(_mixer_b_kernel, att_w=att_w, att_hd=att_hd, n_heads=n_heads, hd=hd),
        grid=(bsz, s // tq),
        in_specs=[
            pl.BlockSpec((None, tq, d), lambda b, i: (b, i, 0)),
            const(1, d),
            const(*w_in.shape),
            const(att_w, tq),
            kspec(0), kspec(1), kspec(2),
            vspec(0), vspec(1), vspec(2),
            const(*pb.shape),
            pl.BlockSpec((None, n_mem, mw), lambda b, i: (b, 0, 0)),
            pl.BlockSpec((None, n_mem, mw), lambda b, i: (b, 0, 0)),
            const(1, hd),
            const(*w_out.shape),
        ],
        out_specs=pl.BlockSpec((None, tq, d), lambda b, i: (b, i, 0)),
        out_shape=jax.ShapeDtypeStruct(x.shape, F32),
        scratch_shapes=[
            pltpu.VMEM((att_w // 2, tq), jnp.uint32),
            pltpu.VMEM((2, 2, KEY_ROWS, LANES), F32),
            pltpu.VMEM((2, 3, tq, tq), BF16),
            pltpu.VMEM((att_w, tq), F32),
        ],
        compiler_params=_cparams(("arbitrary", "arbitrary")),
        name="mixer_b",
    )(x, g, w_in, gqa_row, k, k, k, vt, vt, vt, pb, km, vm, gq, w_out)


def _block_diag_windows(w_r, w_i):
    n, bw, _ = w_r.shape
    width = n * bw
    assert width % MXU_COLS == 0
    reps = -(-MXU_COLS // bw) + 1
    starts, tiles = [], []
    for j in range(width // MXU_COLS):
        c0 = j * MXU_COLS
        lo = (c0 // bw) * bw
        hi = -((-(c0 + MXU_COLS)) // bw) * bw
        ws = min((lo // LANES) * LANES, width - BD_WINDOW)
        assert ws <= lo and hi <= ws + BD_WINDOW
        rows = ws + jnp.arange(BD_WINDOW)[:, None]
        cols = c0 + jnp.arange(MXU_COLS)[None, :]
        on_diag = (rows // bw) == (cols // bw)

        def window(w):
            rep = jnp.tile(w.reshape(width, bw)[ws:ws + BD_WINDOW], (1, reps))[:, c0 % bw:c0 % bw + MXU_COLS]
            return jnp.where(on_diag, rep, 0.0)

        tiles.append(jnp.concatenate([window(w_r), window(w_i)], axis=1))
        starts.append(ws)
    return jnp.stack(tiles).astype(BF16), tuple(starts)


def kernel(x, mem, g_mix, g_mem, w_mem_kv, g_q_mem, g_k_mem, w_in_a, w_conv_a, b_conv_a, w_r_a, b_r_a,
           w_i_a, b_i_a, lam_a, w_out_a, g_kv, w_kv, g_k_att, w_in_b, g_q_att, rel_bias_b, w_out_b,
           g_ffn, w_up, w_dw_ffn, b_dw_ffn, w_down):
    bsz, s, d = x.shape
    depth = g_mix.shape[0]
    n_a = w_in_a.shape[0]
    mem_hd = g_q_mem.shape[1]
    mem_w = w_mem_kv.shape[2] // 2
    mem_heads = mem_w // mem_hd
    att_hd = g_k_att.shape[0]
    att_w = w_kv.shape[1] // 2
    att_heads = att_w // att_hd
    d_rnn = w_conv_a.shape[2]
    assert att_hd * 2 == LANES and att_hd == CHUNK and bsz == SUBLANES
    assert s % TS_KV == 0 and s % TS_MIX == 0 and s % TQ == 0 and s % TT_FFN == 0 and s % TT_RGLRU == 0

    row = lambda v: v.reshape(1, -1).astype(F32)
    km_all, vm_all = _memkv_call(mem, g_mem, w_mem_kv, g_k_mem, mem_heads, mem_hd)

    g_ffn3 = g_ffn[:, None, :].astype(F32)
    b_dw3 = b_dw_ffn[:, None, :].astype(F32)
    w_up_bf = w_up.astype(BF16)
    w_down_bf = w_down.astype(BF16)
    k = vt = None
    for l in range(depth):
        if l < n_a:
            wbd, bd_starts = _block_diag_windows(w_r_a[l], w_i_a[l])
            w_in = w_in_a[l].astype(BF16)
            main = _rglru_call(x, row(g_mix[l]), w_in[:, :2 * d_rnn], w_conv_a[l], row(b_conv_a[l]), wbd,
                               bd_starts, row(b_r_a[l]), row(b_i_a[l]), row(lam_a[l]))
            x = _mixout_call(x, main, row(g_mix[l]), w_in[:, 2 * d_rnn:], km_all[l], vm_all[l],
                             row(g_q_mem[l]), w_out_a[l].astype(BF16), mem_heads, mem_hd)
        else:
            j = l - n_a
            if j == 0:
                k, vt = _kv_call(x, row(g_kv), w_kv[:, :att_w].astype(BF16), w_kv[:, att_w:].T.astype(BF16),
                                 row(jnp.tile(g_k_att, att_heads)), att_hd)
            gqa = jnp.broadcast_to((jnp.tile(g_q_att[j], att_heads) * (att_hd ** -0.5 * LOG2E))[:, None],
                                   (att_w, TQ)).astype(F32)
            x = _mixer_b_call(x, row(g_mix[l]), w_in_b[j].astype(BF16), gqa, k, vt, _bias_call(rel_bias_b[j]),
                              km_all[l], vm_all[l], row(g_q_mem[l]), w_out_b[j].astype(BF16),
                              att_hd, mem_heads, mem_hd)
        x = _ffn_call(x, l, g_ffn3, w_up_bf, w_dw_ffn, b_dw3, w_down_bf)
    return x
```
